```python
import math
import jax, jax.numpy as jnp
from jax import lax
import numpy as np

D_MODEL = 4096
BATCH = 4
SEQ = 4096
DEPTH = 1
DEC_BATCH = 4
DEC_SEQ = 2048
PAST_LEN = 128

PLE_DIM = 256
HEAD_DIM = 128
N_Q_HEADS = D_MODEL // 256
N_KV_HEADS = N_Q_HEADS // 4
Q_PER_KV = N_Q_HEADS // N_KV_HEADS
ATTN_WIDTH = N_Q_HEADS * HEAD_DIM
KV_WIDTH = N_KV_HEADS * HEAD_DIM
WINDOW = 128
BLOCK = 128
N_BUCKETS = 32
MAX_DISTANCE = 128
HYENA_WIDTH = D_MODEL // 2
FILTER_BANDS = 16
FILTER_EMB = 2 * FILTER_BANDS + 1
FILTER_HIDDEN = 64
SHORT_CONV = 3
D_FF = 11008
EPS = 1e-6
NEG_INF = -1e30
Q_END = ATTN_WIDTH
K_END = Q_END + KV_WIDTH
V_END = K_END + KV_WIDTH
HY_END = V_END + 3 * HYENA_WIDTH
IN_COLS = HY_END + 2 * D_MODEL

kernel_name = 'hybrid_swa_hyena_encoder'


def rmsnorm(x, g):
    x32 = x.astype(jnp.float32)
    y = x32 * lax.rsqrt(jnp.mean(x32 * x32, axis=-1, keepdims=True) + EPS)
    return (y * g.astype(jnp.float32)).astype(x.dtype)


def dwconv3(x, w, b):
    xp = jnp.pad(x, ((0, 0), (1, 1), (0, 0)))
    return xp[:, :-2] * w[0] + xp[:, 1:-1] * w[1] + xp[:, 2:] * w[2] + b


def t5_bucket(rel):
    half = N_BUCKETS // 2
    max_exact = half // 2
    ret = jnp.where(rel > 0, half, 0)
    n = jnp.abs(rel)
    nf = jnp.maximum(n, 1).astype(jnp.float32)
    large = max_exact + (jnp.log(nf / max_exact) / math.log(MAX_DISTANCE / max_exact) * (half - max_exact)).astype(jnp.int32)
    large = jnp.minimum(large, half - 1)
    return ret + jnp.where(n < max_exact, n, large)


def windowed_gqa(q, k, v, rel_bias, sink):
    B, L = q.shape[:2]
    nb = L // BLOCK
    qb = q.reshape(B, nb, BLOCK, N_KV_HEADS, Q_PER_KV, HEAD_DIM)

    def band(t):
        tp = jnp.pad(t, ((0, 0), (BLOCK, BLOCK), (0, 0), (0, 0))).reshape(B, nb + 2, BLOCK, N_KV_HEADS, HEAD_DIM)
        return jnp.concatenate([tp[:, :-2], tp[:, 1:-1], tp[:, 2:]], axis=2)

    kb, vb = band(k), band(v)
    s = jnp.einsum('bnqkgd,bnskd->bnkgqs', qb, kb, preferred_element_type=jnp.float32) * (HEAD_DIM ** -0.5)
    qi = jnp.arange(BLOCK)[:, None]
    sj = jnp.arange(3 * BLOCK)[None, :]
    rel = sj - BLOCK - qi
    bias = jnp.transpose(rel_bias[t5_bucket(rel)], (2, 0, 1)).astype(jnp.float32)
    bias = bias.reshape(N_KV_HEADS, Q_PER_KV, BLOCK, 3 * BLOCK)
    kpos = jnp.arange(nb)[:, None] * BLOCK - BLOCK + jnp.arange(3 * BLOCK)[None, :]
    valid = (jnp.abs(rel) <= WINDOW)[None] & ((kpos >= 0) & (kpos < L))[:, None, :]
    s = jnp.where(valid[None, :, None, None], s + bias, NEG_INF)
    sink_l = sink.astype(jnp.float32).reshape(N_KV_HEADS, Q_PER_KV)[:, :, None, None]
    m = jnp.maximum(jnp.max(s, axis=-1, keepdims=True), sink_l)
    pr = jnp.exp(s - m)
    denom = jnp.sum(pr, axis=-1, keepdims=True) + jnp.exp(sink_l - m)
    o = jnp.einsum('bnkgqs,bnskd->bnqkgd', (pr / denom).astype(v.dtype), vb)
    return o.reshape(B, L, ATTN_WIDTH)


def hyena_filters(L, w1, b1, f1, w2, b2, f2, w3, decay):
    f32 = jnp.float32
    pos = jnp.arange(L, dtype=f32)
    t = pos / (L - 1)
    bands = jnp.linspace(1e-4, FILTER_BANDS - 1, FILTER_BANDS, dtype=f32)
    ang = (2.0 * math.pi / L) * pos[:, None] * bands[None, :]
    z = jnp.concatenate([t[:, None], jnp.cos(ang), -jnp.sin(ang)], axis=-1)
    h = jnp.sin(f1.astype(f32) * (z @ w1.astype(f32) + b1.astype(f32)))
    h = jnp.sin(f2.astype(f32) * (h @ w2.astype(f32) + b2.astype(f32)))
    h = (h @ w3.astype(f32)).reshape(L, 2, HYENA_WIDTH)
    return h * jnp.exp(-t[:, None, None] * decay.astype(f32)[None])


def bidir_long_conv(u, h_fwd, h_bwd):
    L, C = h_fwd.shape
    k = jnp.concatenate([h_fwd, jnp.zeros((1, C), jnp.float32), h_bwd[:0:-1]], axis=0)
    kf = jnp.fft.rfft(k, axis=0)
    uf = jnp.fft.rfft(u.astype(jnp.float32), n=2 * L, axis=1)
    return jnp.fft.irfft(uf * kf[None], n=2 * L, axis=1)[:, :L]


def encoder_layer(x, p, rel_bias, g_mix, w_in, attn_sink, hy_short_w, hy_short_b,
                  hy_filt_w1, hy_filt_b1, hy_filt_f1, hy_filt_w2, hy_filt_b2, hy_filt_f2,
                  hy_filt_w3, hy_decay, hy_skip, w_attn_o, w_hyena_o, w_out,
                  g_ffn, w_up, ffn_conv_w, ffn_conv_b, w_down, g_ple, w_ple_gate, w_ple):
    B, L, _ = x.shape
    h = rmsnorm(x, g_mix)
    z = h @ w_in
    q, k, v, hy, gate_logits = jnp.split(z, [Q_END, K_END, V_END, HY_END], axis=-1)
    attn = windowed_gqa(q.reshape(B, L, N_Q_HEADS, HEAD_DIM),
                        k.reshape(B, L, N_KV_HEADS, HEAD_DIM),
                        v.reshape(B, L, N_KV_HEADS, HEAD_DIM), rel_bias, attn_sink)
    hy = dwconv3(hy, hy_short_w, hy_short_b)
    hv, hx1, hx0 = jnp.split(hy, 3, axis=-1)
    filt = hyena_filters(L, hy_filt_w1, hy_filt_b1, hy_filt_f1, hy_filt_w2, hy_filt_b2,
                         hy_filt_f2, hy_filt_w3, hy_decay)
    u = hv * hx1
    hy_out = (bidir_long_conv(u, filt[:, 0], filt[:, 1])
              + u.astype(jnp.float32) * hy_skip.astype(jnp.float32)).astype(x.dtype) * hx0
    gates = jax.nn.sigmoid(gate_logits)
    g_attn, g_hy = jnp.split(gates, 2, axis=-1)
    mixed = (g_attn * (attn @ w_attn_o) + g_hy * (hy_out @ w_hyena_o)) @ w_out
    x = x + mixed
    h = rmsnorm(x, g_ffn)
    gu, val = jnp.split(h @ w_up, 2, axis=-1)
    gu = dwconv3(gu, ffn_conv_w, ffn_conv_b)
    x = x + (jax.nn.gelu(gu, approximate=False) * val) @ w_down
    h = rmsnorm(x, g_ple)
    x = x + jax.nn.sigmoid(h @ w_ple_gate) * (p @ w_ple)
    return x


def trunk(x, p, rel_bias, layer_weights, g_final):
    for l in range(DEPTH):
        x = encoder_layer(x, p[l], rel_bias, *[w[l] for w in layer_weights])
    return rmsnorm(x, g_final)


def setup_inputs(seed: int = 0) -> dict:
    key = jax.random.key(seed)
    ks = jax.random.split(key, 32)
    f32 = jnp.float32

    def nrm(k, shape, scale):
        return jax.random.normal(k, shape, f32) * scale

    def gain(k, shape):
        return 1.0 + 0.02 * jax.random.normal(k, shape, f32)

    return {
        'x_prompt': nrm(ks[0], (BATCH, SEQ, D_MODEL), 1.0),
        'x_sample': nrm(ks[1], (DEC_BATCH, DEC_SEQ, D_MODEL), 1.0),
        'p_prompt': nrm(ks[2], (DEPTH, BATCH, SEQ, PLE_DIM), 1.0),
        'p_sample': nrm(ks[3], (DEPTH, DEC_BATCH, DEC_SEQ, PLE_DIM), 1.0),
        'rel_bias': nrm(ks[4], (N_BUCKETS, N_Q_HEADS), 0.5),
        'g_mix': gain(ks[5], (DEPTH, D_MODEL)),
        'w_in': nrm(ks[6], (DEPTH, D_MODEL, IN_COLS), D_MODEL ** -0.5),
        'attn_sink': nrm(ks[7], (DEPTH, N_Q_HEADS), 0.5),
        'hy_short_w': nrm(ks[8], (DEPTH, SHORT_CONV, 3 * HYENA_WIDTH), SHORT_CONV ** -0.5),
        'hy_short_b': nrm(ks[9], (DEPTH, 3 * HYENA_WIDTH), 0.02),
        'hy_filt_w1': nrm(ks[10], (DEPTH, FILTER_EMB, FILTER_HIDDEN), FILTER_EMB ** -0.5),
        'hy_filt_b1': nrm(ks[11], (DEPTH, FILTER_HIDDEN), 0.02),
        'hy_filt_f1': gain(ks[12], (DEPTH, FILTER_HIDDEN)),
        'hy_filt_w2': nrm(ks[13], (DEPTH, FILTER_HIDDEN, FILTER_HIDDEN), FILTER_HIDDEN ** -0.5),
        'hy_filt_b2': nrm(ks[14], (DEPTH, FILTER_HIDDEN), 0.02),
        'hy_filt_f2': gain(ks[15], (DEPTH, FILTER_HIDDEN)),
        'hy_filt_w3': nrm(ks[16], (DEPTH, FILTER_HIDDEN, 2 * HYENA_WIDTH), 0.03 * FILTER_HIDDEN ** -0.5),
        'hy_decay': jax.random.uniform(ks[17], (DEPTH, 2, HYENA_WIDTH), f32, 3.0, 15.0),
        'hy_skip': nrm(ks[18], (DEPTH, HYENA_WIDTH), 0.5),
        'w_attn_o': nrm(ks[19], (DEPTH, ATTN_WIDTH, D_MODEL), ATTN_WIDTH ** -0.5),
        'w_hyena_o': nrm(ks[20], (DEPTH, HYENA_WIDTH, D_MODEL), HYENA_WIDTH ** -0.5),
        'w_out': nrm(ks[21], (DEPTH, D_MODEL, D_MODEL), D_MODEL ** -0.5),
        'g_ffn': gain(ks[22], (DEPTH, D_MODEL)),
        'w_up': nrm(ks[23], (DEPTH, D_MODEL, 2 * D_FF), D_MODEL ** -0.5),
        'ffn_conv_w': nrm(ks[24], (DEPTH, SHORT_CONV, D_FF), SHORT_CONV ** -0.5),
        'ffn_conv_b': nrm(ks[25], (DEPTH, D_FF), 0.02),
        'w_down': nrm(ks[26], (DEPTH, D_FF, D_MODEL), D_FF ** -0.5),
        'g_ple': gain(ks[27], (DEPTH, D_MODEL)),
        'w_ple_gate': nrm(ks[28], (DEPTH, D_MODEL, D_MODEL), D_MODEL ** -0.5),
        'w_ple': nrm(ks[29], (DEPTH, PLE_DIM, D_MODEL), PLE_DIM ** -0.5),
        'g_final': gain(ks[30], (D_MODEL,)),
    }


def reference(x_prompt, x_sample, p_prompt, p_sample, rel_bias, g_mix, w_in, attn_sink,
              hy_short_w, hy_short_b, hy_filt_w1, hy_filt_b1, hy_filt_f1, hy_filt_w2,
              hy_filt_b2, hy_filt_f2, hy_filt_w3, hy_decay, hy_skip, w_attn_o, w_hyena_o,
              w_out, g_ffn, w_up, ffn_conv_w, ffn_conv_b, w_down, g_ple, w_ple_gate, w_ple,
              g_final):
    layer_weights = (g_mix, w_in, attn_sink, hy_short_w, hy_short_b, hy_filt_w1, hy_filt_b1,
                     hy_filt_f1, hy_filt_w2, hy_filt_b2, hy_filt_f2, hy_filt_w3, hy_decay,
                     hy_skip, w_attn_o, w_hyena_o, w_out, g_ffn, w_up, ffn_conv_w, ffn_conv_b,
                     w_down, g_ple, w_ple_gate, w_ple)
    y_prompt = trunk(x_prompt, p_prompt, rel_bias, layer_weights, g_final)
    y_sample = trunk(x_sample, p_sample, rel_bias, layer_weights, g_final)
    return (y_prompt, y_sample)
```

```python
import functools
import math

import jax
import jax.numpy as jnp
from jax import lax
from jax.experimental import pallas as pl
from jax.experimental.pallas import tpu as pltpu

F32 = jnp.float32
BF16 = jnp.bfloat16

EPS = 1e-6
NEG_INF = -1e30
HEAD_DIM = 128
WINDOW = 128
BLOCK = 128
N_BUCKETS = 32
MAX_DISTANCE = 128
FILTER_BANDS = 16

V7X_VMEM_LIMIT_BYTES = 56 * 1024 * 1024
LANES = 128
BF16_SUBLANES = 16
FF_PAD = 512


def _pick(n, pref, align):
    t = min(pref, n)
    t -= t % align
    while t >= align:
        if n % t == 0:
            return t
        t -= align
    return n


def _params(sem):
    return pltpu.CompilerParams(dimension_semantics=sem, vmem_limit_bytes=V7X_VMEM_LIMIT_BYTES)


def _rmsnorm_kernel(x_ref, g_ref, o_ref):
    x = x_ref[...].astype(F32)
    ms = jnp.mean(x * x, axis=-1, keepdims=True)
    o_ref[...] = (x * lax.rsqrt(ms + EPS) * g_ref[...]).astype(o_ref.dtype)


def _rmsnorm(x, g, out_dtype):
    rows, d = x.shape
    tm = _pick(rows, 512, 8)
    return pl.pallas_call(
        _rmsnorm_kernel,
        grid=(rows // tm,),
        in_specs=[pl.BlockSpec((tm, d), lambda i: (i, 0)), pl.BlockSpec((1, d), lambda i: (0, 0))],
        out_specs=pl.BlockSpec((tm, d), lambda i: (i, 0)),
        out_shape=jax.ShapeDtypeStruct((rows, d), out_dtype),
        compiler_params=_params(("parallel",)),
        name="rmsnorm",
    )(x, g.reshape(1, d).astype(F32))


def _fused_matmul_kernel(*refs, n_pairs, epilogue):
    o_ref = refs[-1]
    dots = [jnp.dot(refs[2 * p][...], refs[2 * p + 1][...], preferred_element_type=F32)
            for p in range(n_pairs)]
    extras = [r[...] for r in refs[2 * n_pairs:-1]]
    o_ref[...] = epilogue(dots, extras).astype(o_ref.dtype)


def _fused_matmul(pairs, extras, epilogue, out_dtype, n_cols, tm, tn, name):
    rows = pairs[0][0].shape[0]
    assert rows % tm == 0 and n_cols % tn == 0
    args, in_specs = [], []
    for a, b in pairs:
        k = a.shape[1]
        assert b.shape[0] == k
        args += [a, b]
        in_specs += [pl.BlockSpec((tm, k), lambda i, j: (i, 0)), pl.BlockSpec((k, tn), lambda i, j: (0, j))]
    for arr, kind, off in extras:
        assert off % tn == 0
        ob = off // tn
        args.append(arr)
        if kind == "tile":
            in_specs.append(pl.BlockSpec((tm, tn), lambda i, j, ob=ob: (i, j + ob)))
        else:
            in_specs.append(pl.BlockSpec((1, tn), lambda i, j, ob=ob: (0, j + ob)))
    return pl.pallas_call(
        functools.partial(_fused_matmul_kernel, n_pairs=len(pairs), epilogue=epilogue),
        grid=(rows // tm, n_cols // tn),
        in_specs=in_specs,
        out_specs=pl.BlockSpec((tm, tn), lambda i, j: (i, j)),
        out_shape=jax.ShapeDtypeStruct((rows, n_cols), out_dtype),
        compiler_params=_params(("parallel", "parallel")),
        name=name,
    )(*args)


def _ep_plain(dots, extras):
    return dots[0]


def _ep_branch_merge(dots, extras):
    ga, gh = extras
    return jax.nn.sigmoid(ga.astype(F32)) * dots[0] + jax.nn.sigmoid(gh.astype(F32)) * dots[1]


def _ep_residual(dots, extras):
    return extras[0] + dots[0]


def _ep_ple(dots, extras):
    return extras[0] + jax.nn.sigmoid(dots[0]) * dots[1]


def _bias_table_kernel(rb_ref, bkt_ref, o_ref):
    h = pl.program_id(0)
    bkt = bkt_ref[...]
    q = lax.broadcasted_iota(jnp.int32, bkt.shape, 0)
    s = lax.broadcasted_iota(jnp.int32, bkt.shape, 1)
    rel = s - BLOCK - q
    acc = jnp.zeros(bkt.shape, F32)
    for b in range(N_BUCKETS):
        acc = jnp.where(bkt == b, rb_ref[b, h], acc)
    o_ref[0] = jnp.where(jnp.abs(rel) <= WINDOW, acc, NEG_INF)


def _bias_table(rel_bias):
    n_heads = rel_bias.shape[1]
    half = N_BUCKETS // 2
    max_exact = half // 2
    qi = jnp.arange(BLOCK)[:, None]
    sj = jnp.arange(3 * BLOCK)[None, :]
    rel = sj - BLOCK - qi
    ret = jnp.where(rel > 0, half, 0)
    n = jnp.abs(rel)
    nf = jnp.maximum(n, 1).astype(F32)
    large = max_exact + (jnp.log(nf / max_exact) / math.log(MAX_DISTANCE / max_exact)
                         * (half - max_exact)).astype(jnp.int32)
    large = jnp.minimum(large, half - 1)
    bucket = (ret + jnp.where(n < max_exact, n, large)).astype(jnp.int32)
    return pl.pallas_call(
        _bias_table_kernel,
        grid=(n_heads,),
        in_specs=[pl.BlockSpec(memory_space=pltpu.SMEM),
                  pl.BlockSpec((BLOCK, 3 * BLOCK), lambda h: (0, 0))],
        out_specs=pl.BlockSpec((1, BLOCK, 3 * BLOCK), lambda h: (h, 0, 0)),
        out_shape=jax.ShapeDtypeStruct((n_heads, BLOCK, 3 * BLOCK), F32),
        compiler_params=_params(("arbitrary",)),
        name="attn_bias_table",
    )(rel_bias.astype(F32), bucket)


def _attn_kernel(sink_ref, q_ref, kp_ref, kc_ref, kn_ref, vp_ref, vc_ref, vn_ref, tab_ref, o_ref,
                 *, nb, n_kv, q_per_kv):
    n = pl.program_id(0) % nb
    col = lax.broadcasted_iota(jnp.int32, (1, 3 * BLOCK), 1)
    outside = ((col < BLOCK) & (n == 0)) | ((col >= 2 * BLOCK) & (n == nb - 1))
    scale = HEAD_DIM ** -0.5
    for kh in range(n_kv):
        ks = slice(kh * HEAD_DIM, (kh + 1) * HEAD_DIM)
        kband = jnp.concatenate([kp_ref[:, ks], kc_ref[:, ks], kn_ref[:, ks]], axis=0)
        vband = jnp.concatenate([vp_ref[:, ks], vc_ref[:, ks], vn_ref[:, ks]], axis=0)
        heads = range(kh * q_per_kv, (kh + 1) * q_per_kv)
        qh = jnp.concatenate([q_ref[:, h * HEAD_DIM:(h + 1) * HEAD_DIM] for h in heads], axis=0)
        s = lax.dot_general(qh, kband, (((1,), (1,)), ((), ())), preferred_element_type=F32) * scale
        s = s + tab_ref[kh * q_per_kv:(kh + 1) * q_per_kv].reshape(q_per_kv * BLOCK, 3 * BLOCK)
        s = jnp.where(outside, NEG_INF, s)
        sink = jnp.concatenate([jnp.full((BLOCK, 1), sink_ref[h], F32) for h in heads], axis=0)
        m = jnp.maximum(jnp.max(s, axis=-1, keepdims=True), sink)
        p = jnp.exp(s - m)
        denom = jnp.sum(p, axis=-1, keepdims=True) + jnp.exp(sink - m)
        o = jnp.dot(p.astype(BF16), vband, preferred_element_type=F32) / denom
        for g, h in enumerate(heads):
            o_ref[:, h * HEAD_DIM:(h + 1) * HEAD_DIM] = o[g * BLOCK:(g + 1) * BLOCK].astype(o_ref.dtype)


def _attention(z, tab, sink, seq_len, attn_w, kv_w):
    rows = z.shape[0]
    nb = seq_len // BLOCK
    nblk = rows // BLOCK
    n_kv = kv_w // HEAD_DIM
    q_per_kv = attn_w // kv_w
    assert attn_w % kv_w == 0
    kcol = attn_w // kv_w
    vcol = kcol + 1

    def prev(c):
        return lambda r: (jnp.maximum(r - 1, 0), c)

    def cur(c):
        return lambda r: (r, c)

    def nxt(c):
        return lambda r: (jnp.minimum(r + 1, nblk - 1), c)

    kv_spec = lambda f, c: pl.BlockSpec((BLOCK, kv_w), f(c))
    return pl.pallas_call(
        functools.partial(_attn_kernel, nb=nb, n_kv=n_kv, q_per_kv=q_per_kv),
        grid=(nblk,),
        in_specs=[pl.BlockSpec(memory_space=pltpu.SMEM),
                  pl.BlockSpec((BLOCK, attn_w), lambda r: (r, 0)),
                  kv_spec(prev, kcol), kv_spec(cur, kcol), kv_spec(nxt, kcol),
                  kv_spec(prev, vcol), kv_spec(cur, vcol), kv_spec(nxt, vcol),
                  pl.BlockSpec(tab.shape, lambda r: (0, 0, 0))],
        out_specs=pl.BlockSpec((BLOCK, attn_w), lambda r: (r, 0)),
        out_shape=jax.ShapeDtypeStruct((rows, attn_w), BF16),
        compiler_params=_params(("parallel",)),
        name="windowed_gqa",
    )(sink.astype(F32), z, z, z, z, z, z, z, tab)


def _conv3_full(x, w, b, rows, n_rows):
    xm = jnp.where(rows == 0, 0.0, pltpu.roll(x, 1, 0))
    xp = jnp.where(rows == n_rows - 1, 0.0, pltpu.roll(x, n_rows - 1, 0))
    return xm * w[0:1] + x * w[1:2] + xp * w[2:3] + b


def _hy_pre_kernel(v_ref, x1_ref, x0_ref, wv_ref, w1_ref, w0_ref, bv_ref, b1_ref, b0_ref, u_ref, x0c_ref):
    n_rows = v_ref.shape[1]
    rows = lax.broadcasted_iota(jnp.int32, (n_rows, 1), 0)
    hv = _conv3_full(v_ref[0].astype(F32), wv_ref[...], bv_ref[...], rows, n_rows)
    hx1 = _conv3_full(x1_ref[0].astype(F32), w1_ref[...], b1_ref[...], rows, n_rows)
    u_ref[0] = (hv * hx1).astype(u_ref.dtype)
    hx0 = _conv3_full(x0_ref[0].astype(F32), w0_ref[...], b0_ref[...], rows, n_rows)
    x0c_ref[0] = hx0.astype(x0c_ref.dtype)


def _hy_pre(z3, short_w, short_b, hy_off, hw):
    bsz, seq_len, _ = z3.shape
    tc = LANES
    assert hy_off % tc == 0 and hw % tc == 0
    ob, nc = hy_off // tc, hw // tc
    zspec = lambda part: pl.BlockSpec((1, seq_len, tc), lambda b, c: (b, 0, ob + part * nc + c))
    wspec = lambda part: pl.BlockSpec((3, tc), lambda b, c: (0, part * nc + c))
    bspec = lambda part: pl.BlockSpec((1, tc), lambda b, c: (0, part * nc + c))
    ospec = pl.BlockSpec((1, seq_len, tc), lambda b, c: (b, 0, c))
    sw = short_w.astype(F32)
    sb = short_b.reshape(1, -1).astype(F32)
    return pl.pallas_call(
        _hy_pre_kernel,
        grid=(bsz, nc),
        in_specs=[zspec(0), zspec(1), zspec(2), wspec(0), wspec(1), wspec(2), bspec(0), bspec(1), bspec(2)],
        out_specs=[ospec, ospec],
        out_shape=[jax.ShapeDtypeStruct((bsz, seq_len, hw), BF16)] * 2,
        compiler_params=_params(("parallel", "parallel")),
        name="hyena_short_conv",
    )(z3, z3, z3, sw, sw, sw, sb, sb, sb)


def _filter_mlp_kernel(zf_ref, w1_ref, b1_ref, f1_ref, w2_ref, b2_ref, f2_ref, w3f_ref, w3b_ref,
                       decf_ref, decb_ref, hs_ref, hd_ref):
    tl = zf_ref.shape[0]
    hp = lax.Precision.HIGHEST
    zf = zf_ref[...]
    h = jnp.sin(f1_ref[...] * (jnp.dot(zf, w1_ref[...], precision=hp, preferred_element_type=F32) + b1_ref[...]))
    h = jnp.sin(f2_ref[...] * (jnp.dot(h, w2_ref[...], precision=hp, preferred_element_type=F32) + b2_ref[...]))
    t = zf[:, 0:1]
    fwd = jnp.dot(h, w3f_ref[...], precision=hp, preferred_element_type=F32) * jnp.exp(-t * decf_ref[...])
    bwd = jnp.dot(h, w3b_ref[...], precision=hp, preferred_element_type=F32) * jnp.exp(-t * decb_ref[...])
    pos = pl.program_id(0) * tl + lax.broadcasted_iota(jnp.int32, (tl, 1), 0)
    bwd = jnp.where(pos == 0, 0.0, bwd)
    hs_ref[...] = (fwd + bwd).astype(hs_ref.dtype)
    hd_ref[...] = (fwd - bwd).astype(hd_ref.dtype)


def _pad2(a, rows, cols):
    return jnp.pad(a.astype(F32), ((0, rows - a.shape[0]), (0, cols - a.shape[1])))


def _filter_mlp(seq_len, w1, b1, f1, w2, b2, f2, w3, decay):
    hw = decay.shape[1]
    pos = jnp.arange(seq_len, dtype=F32)
    t = pos / (seq_len - 1)
    bands = jnp.linspace(1e-4, FILTER_BANDS - 1, FILTER_BANDS, dtype=F32)
    ang = (2.0 * math.pi / seq_len) * pos[:, None] * bands[None, :]
    zf = jnp.concatenate([t[:, None], jnp.cos(ang), -jnp.sin(ang)], axis=-1)
    emb, hid = w1.shape
    assert emb <= LANES and hid <= LANES
    zf = _pad2(zf, seq_len, LANES)
    row = lambda v: _pad2(v.reshape(1, -1), 1, LANES)
    w3p = _pad2(w3, LANES, 2 * hw)
    dec = decay.reshape(1, 2 * hw).astype(F32)
    tl = _pick(seq_len, 512, 8)
    tc = _pick(hw, 512, LANES)
    nc = hw // tc
    small = lambda shape: pl.BlockSpec(shape, lambda i, c: (0, 0))
    return pl.pallas_call(
        _filter_mlp_kernel,
        grid=(seq_len // tl, nc),
        in_specs=[pl.BlockSpec((tl, LANES), lambda i, c: (i, 0)),
                  small((LANES, LANES)), small((1, LANES)), small((1, LANES)),
                  small((LANES, LANES)), small((1, LANES)), small((1, LANES)),
                  pl.BlockSpec((LANES, tc), lambda i, c: (0, c)),
                  pl.BlockSpec((LANES, tc), lambda i, c: (0, nc + c)),
                  pl.BlockSpec((1, tc), lambda i, c: (0, c)),
                  pl.BlockSpec((1, tc), lambda i, c: (0, nc + c))],
        out_specs=[pl.BlockSpec((tl, tc), lambda i, c: (i, c))] * 2,
        out_shape=[jax.ShapeDtypeStruct((seq_len, hw), BF16)] * 2,
        compiler_params=_params(("parallel", "parallel")),
        name="hyena_filter_mlp",
    )(zf, _pad2(w1, LANES, LANES), row(b1), row(f1), _pad2(w2, LANES, LANES), row(b2), row(f2),
      w3p, w3p, dec, dec)


def _dft_mats(seq_len):
    n2 = 2 * seq_len
    k = jnp.arange(seq_len, dtype=jnp.int32)[:, None]
    n = jnp.arange(seq_len, dtype=jnp.int32)[None, :]
    ang = ((k * n) % n2).astype(F32) * (2.0 * math.pi / n2)
    fc = jnp.cos(ang)
    sn = jnp.sin(ang)
    alt_n = jnp.where(n % 2 == 1, -1.0, 1.0).astype(F32)
    alt_k = jnp.where(k % 2 == 1, -1.0, 1.0).astype(F32)
    fs = jnp.where(k == 0, alt_n, sn)
    fst = jnp.where(n == 0, alt_k, sn)
    return fc.astype(BF16), fs.astype(BF16), fst.astype(BF16)


def _filter_spectrum_kernel(fc_ref, fs_ref, hs_ref, hd_ref, a_ref, d_ref, *, seq_len):
    tm = fc_ref.shape[0]
    hs = hs_ref[...]
    row0 = (pl.program_id(0) * tm + lax.broadcasted_iota(jnp.int32, (tm, 1), 0)) == 0
    scale = jnp.where(row0, 1.0 / (2 * seq_len), 2.0 / (2 * seq_len))
    a_ref[...] = scale * jnp.dot(fc_ref[...], hs, preferred_element_type=F32)
    fs = fs_ref[...]
    d = jnp.where(row0, jnp.dot(fs, hs, preferred_element_type=F32),
                  jnp.dot(fs, hd_ref[...], preferred_element_type=F32))
    d_ref[...] = scale * d


def _filter_spectrum(fc, fs, hs, hd):
    seq_len, hw = hs.shape
    tm = _pick(seq_len, 512, BF16_SUBLANES)
    tn = _pick(hw, 512, LANES)
    fspec = pl.BlockSpec((tm, seq_len), lambda i, j: (i, 0))
    hspec = pl.BlockSpec((seq_len, tn), lambda i, j: (0, j))
    ospec = pl.BlockSpec((tm, tn), lambda i, j: (i, j))
    return pl.pallas_call(
        functools.partial(_filter_spectrum_kernel, seq_len=seq_len),
        grid=(seq_len // tm, hw // tn),
        in_specs=[fspec, fspec, hspec, hspec],
        out_specs=[ospec, ospec],
        out_shape=[jax.ShapeDtypeStruct((seq_len, hw), F32)] * 2,
        compiler_params=_params(("parallel", "parallel")),
        name="hyena_filter_spectrum",
    )(fc, fs, hs, hd)


def _dft_fwd_kernel(fc_ref, fs_ref, u_ref, a_ref, d_ref, w1_ref, w2_ref):
    tm = fc_ref.shape[0]
    u = u_ref[0]
    p = jnp.dot(fc_ref[...], u, preferred_element_type=F32)
    q = jnp.dot(fs_ref[...], u, preferred_element_type=F32)
    a = a_ref[...]
    d = d_ref[...]
    row0 = (pl.program_id(0) * tm + lax.broadcasted_iota(jnp.int32, (tm, 1), 0)) == 0
    qd = q * d
    w1_ref[0] = (p * a - jnp.where(row0, 0.0, qd)).astype(w1_ref.dtype)
    w2_ref[0] = jnp.where(row0, qd, p * d + q * a).astype(w2_ref.dtype)


def _dft_fwd(fc, fs, u, a, d):
    bsz, seq_len, hw = u.shape
    tm = _pick(seq_len, 512, BF16_SUBLANES)
    tn = _pick(hw, 512, LANES)
    fspec = pl.BlockSpec((tm, seq_len), lambda i, j, b: (i, 0))
    kspec = pl.BlockSpec((tm, tn), lambda i, j, b: (i, j))
    ospec = pl.BlockSpec((1, tm, tn), lambda i, j, b: (b, i, j))
    return pl.pallas_call(
        _dft_fwd_kernel,
        grid=(seq_len // tm, hw // tn, bsz),
        in_specs=[fspec, fspec, pl.BlockSpec((1, seq_len, tn), lambda i, j, b: (b, 0, j)), kspec, kspec],
        out_specs=[ospec, ospec],
        out_shape=[jax.ShapeDtypeStruct((bsz, seq_len, hw), BF16)] * 2,
        compiler_params=_params(("parallel", "parallel", "parallel")),
        name="hyena_dft_forward",
    )(fc, fs, u, a, d)


def _dft_inv_kernel(fc_ref, fst_ref, w1_ref, w2_ref, u_ref, x0c_ref, skip_ref, o_ref):
    y = (jnp.dot(fc_ref[...], w1_ref[0], preferred_element_type=F32)
         + jnp.dot(fst_ref[...], w2_ref[0], preferred_element_type=F32))
    y = y + u_ref[0].astype(F32) * skip_ref[...]
    o_ref[0] = (y * x0c_ref[0].astype(F32)).astype(o_ref.dtype)


def _dft_inv(fc, fst, w1, w2, u, x0c, skip):
    bsz, seq_len, hw = u.shape
    tm = _pick(seq_len, 512, BF16_SUBLANES)
    tn = _pick(hw, 512, LANES)
    fspec = pl.BlockSpec((tm, seq_len), lambda i, j, b: (i, 0))
    wspec = pl.BlockSpec((1, seq_len, tn), lambda i, j, b: (b, 0, j))
    tspec = pl.BlockSpec((1, tm, tn), lambda i, j, b: (b, i, j))
    return pl.pallas_call(
        _dft_inv_kernel,
        grid=(seq_len // tm, hw // tn, bsz),
        in_specs=[fspec, fspec, wspec, wspec, tspec, tspec, pl.BlockSpec((1, tn), lambda i, j, b: (0, j))],
        out_specs=tspec,
        out_shape=jax.ShapeDtypeStruct((bsz, seq_len, hw), BF16),
        compiler_params=_params(("parallel", "parallel", "parallel")),
        name="hyena_dft_inverse",
    )(fc, fst, w1, w2, u, x0c, skip.reshape(1, hw).astype(F32))


def _ffn_up_kernel(h_ref, hp_ref, hn_ref, wg_ref, wv_ref, cw_ref, cb_ref, o_ref, *, seq_len):
    tm = h_ref.shape[0]
    h = h_ref[...]
    wg = wg_ref[...]
    gu = jnp.dot(h, wg, preferred_element_type=F32)
    val = jnp.dot(h, wv_ref[...], preferred_element_type=F32)
    g_prev = jnp.dot(hp_ref[...], wg, preferred_element_type=F32)[BF16_SUBLANES - 1:BF16_SUBLANES]
    g_next = jnp.dot(hn_ref[...], wg, preferred_element_type=F32)[0:1]
    row_start = pl.program_id(0) * tm
    g_prev = jnp.where(row_start % seq_len == 0, 0.0, g_prev)
    g_next = jnp.where((row_start + tm) % seq_len == 0, 0.0, g_next)
    rows = lax.broadcasted_iota(jnp.int32, (tm, 1), 0)
    g_m1 = jnp.where(rows == 0, g_prev, pltpu.roll(gu, 1, 0))
    g_p1 = jnp.where(rows == tm - 1, g_next, pltpu.roll(gu, tm - 1, 0))
    cw = cw_ref[...]
    conv = g_m1 * cw[0:1] + gu * cw[1:2] + g_p1 * cw[2:3] + cb_ref[...]
    gelu = 0.5 * conv * (1.0 + lax.erf(conv * math.sqrt(0.5)))
    o_ref[...] = (gelu * val).astype(o_ref.dtype)


def _ffn_up(h, wg, wv, conv_w, conv_b, seq_len):
    rows, d = h.shape
    ffp = wg.shape[1]
    tm = _pick(seq_len, 1024, BF16_SUBLANES)
    tn = _pick(ffp, 256, LANES)
    hb = tm // BF16_SUBLANES
    n_halo = rows // BF16_SUBLANES
    return pl.pallas_call(
        functools.partial(_ffn_up_kernel, seq_len=seq_len),
        grid=(rows // tm, ffp // tn),
        in_specs=[pl.BlockSpec((tm, d), lambda i, j: (i, 0)),
                  pl.BlockSpec((BF16_SUBLANES, d), lambda i, j: (jnp.maximum(i * hb - 1, 0), 0)),
                  pl.BlockSpec((BF16_SUBLANES, d), lambda i, j: (jnp.minimum((i + 1) * hb, n_halo - 1), 0)),
                  pl.BlockSpec((d, tn), lambda i, j: (0, j)),
                  pl.BlockSpec((d, tn), lambda i, j: (0, j)),
                  pl.BlockSpec((3, tn), lambda i, j: (0, j)),
                  pl.BlockSpec((1, tn), lambda i, j: (0, j))],
        out_specs=pl.BlockSpec((tm, tn), lambda i, j: (i, j)),
        out_shape=jax.ShapeDtypeStruct((rows, ffp), BF16),
        compiler_params=_params(("parallel", "parallel")),
        name="convglu_up",
    )(h, h, h, wg, wv, conv_w, conv_b)


def _prep_layer(w):
    d_model = w["w_out"].shape[0]
    d_ff = w["w_down"].shape[0]
    ffp = -(-d_ff // FF_PAD) * FF_PAD
    pad_c = lambda a: jnp.pad(a, ((0, 0), (0, ffp - d_ff)))
    return dict(
        w_in=w["w_in"].astype(BF16),
        w_attn_o=w["w_attn_o"].astype(BF16),
        w_hyena_o=w["w_hyena_o"].astype(BF16),
        w_out=w["w_out"].astype(BF16),
        w_gu=pad_c(w["w_up"][:, :d_ff]).astype(BF16),
        w_val=pad_c(w["w_up"][:, d_ff:]).astype(BF16),
        ffn_conv_w=pad_c(w["ffn_conv_w"].astype(F32)),
        ffn_conv_b=pad_c(w["ffn_conv_b"].reshape(1, d_ff).astype(F32)),
        w_down=jnp.pad(w["w_down"], ((0, ffp - d_ff), (0, 0))).astype(BF16),
        w_ple_gate=w["w_ple_gate"].astype(BF16),
        w_ple=w["w_ple"].astype(BF16),
        d_model=d_model,
    )


def _encoder_layer(x, p, tab, w, wp, seq_len):
    rows, d_model = x.shape
    bsz = rows // seq_len
    attn_w = w["w_attn_o"].shape[0]
    hw = w["w_hyena_o"].shape[0]
    in_cols = w["w_in"].shape[1]
    kv_w = (in_cols - attn_w - 3 * hw - 2 * d_model) // 2
    hy_off = attn_w + 2 * kv_w
    gate_off = hy_off + 3 * hw
    tm = _pick(seq_len, 1024, BF16_SUBLANES)

    h = _rmsnorm(x, w["g_mix"], BF16)
    z = _fused_matmul([(h, wp["w_in"])], [], _ep_plain, BF16, in_cols, tm, _pick(in_cols, 1024, LANES), "in_proj")
    attn = _attention(z, tab, w["attn_sink"], seq_len, attn_w, kv_w)

    u, x0c = _hy_pre(z.reshape(bsz, seq_len, in_cols), w["hy_short_w"], w["hy_short_b"], hy_off, hw)
    hs, hd = _filter_mlp(seq_len, w["hy_filt_w1"], w["hy_filt_b1"], w["hy_filt_f1"], w["hy_filt_w2"],
                         w["hy_filt_b2"], w["hy_filt_f2"], w["hy_filt_w3"], w["hy_decay"])
    fc, fs, fst = _dft_mats(seq_len)
    ka, kd = _filter_spectrum(fc, fs, hs, hd)
    w1, w2 = _dft_fwd(fc, fs, u, ka, kd)
    hy_out = _dft_inv(fc, fst, w1, w2, u, x0c, w["hy_skip"]).reshape(rows, hw)

    tn = _pick(math.gcd(d_model, gate_off), 512, LANES)
    merged = _fused_matmul([(attn, wp["w_attn_o"]), (hy_out, wp["w_hyena_o"])],
                           [(z, "tile", gate_off), (z, "tile", gate_off + d_model)],
                           _ep_branch_merge, BF16, d_model, tm, tn, "branch_merge")
    tn = _pick(d_model, 512, LANES)
    x = _fused_matmul([(merged, wp["w_out"])], [(x, "tile", 0)], _ep_residual, F32, d_model, tm, tn, "out_proj")

    h = _rmsnorm(x, w["g_ffn"], BF16)
    act = _ffn_up(h, wp["w_gu"], wp["w_val"], wp["ffn_conv_w"], wp["ffn_conv_b"], seq_len)
    x = _fused_matmul([(act, wp["w_down"])], [(x, "tile", 0)], _ep_residual, F32, d_model,
                      _pick(seq_len, 512, BF16_SUBLANES), _pick(d_model, 256, LANES), "ffn_down")

    h = _rmsnorm(x, w["g_ple"], BF16)
    x = _fused_matmul([(h, wp["w_ple_gate"]), (p.astype(BF16), wp["w_ple"])], [(x, "tile", 0)],
                      _ep_ple, F32, d_model, tm, tn, "ple_gate")
    return x


_LAYER_WEIGHT_NAMES = ("g_mix", "w_in", "attn_sink", "hy_short_w", "hy_short_b", "hy_filt_w1", "hy_filt_b1",
                       "hy_filt_f1", "hy_filt_w2", "hy_filt_b2", "hy_filt_f2", "hy_filt_w3", "hy_decay",
                       "hy_skip", "w_attn_o", "w_hyena_o", "w_out", "g_ffn", "w_up", "ffn_conv_w", "ffn_conv_b",
                       "w_down", "g_ple", "w_ple_gate", "w_ple")


def _trunk(x, p, tab, layers, preps, g_final):
    bsz, seq_len, d_model = x.shape
    xf = x.reshape(bsz * seq_len, d_model)
    for l, (w, wp) in enumerate(zip(layers, preps)):
        xf = _encoder_layer(xf, p[l].reshape(bsz * seq_len, -1), tab, w, wp, seq_len)
    return _rmsnorm(xf, g_final, x.dtype).reshape(bsz, seq_len, d_model)


def kernel(x_prompt, x_sample, p_prompt, p_sample, rel_bias, g_mix, w_in, attn_sink, hy_short_w, hy_short_b, hy_filt_w1, hy_filt_b1, hy_filt_f1, hy_filt_w2, hy_filt_b2, hy_filt_f2, hy_filt_w3, hy_decay, hy_skip, w_attn_o, w_hyena_o, w_out, g_ffn, w_up, ffn_conv_w, ffn_conv_b, w_down, g_ple, w_ple_gate, w_ple, g_final):
    stacked = (g_mix, w_in, attn_sink, hy_short_w, hy_short_b, hy_filt_w1, hy_filt_b1, hy_filt_f1, hy_filt_w2,
               hy_filt_b2, hy_filt_f2, hy_filt_w3, hy_decay, hy_skip, w_attn_o, w_hyena_o, w_out, g_ffn, w_up,
               ffn_conv_w, ffn_conv_b, w_down, g_ple, w_ple_gate, w_ple)
    depth = g_mix.shape[0]
    layers = [dict(zip(_LAYER_WEIGHT_NAMES, [a[l] for a in stacked])) for l in range(depth)]
    preps = [_prep_layer(w) for w in layers]
    tab = _bias_table(rel_bias)
    y_prompt = _trunk(x_prompt, p_prompt, tab, layers, preps, g_final)
    y_sample = _trunk(x_sample, p_sample, tab, layers, preps, g_final)
    return (y_prompt, y_sample)
```

```python
import functools
import math

import jax
import jax.numpy as jnp
from jax import lax
from jax.experimental import pallas as pl
from jax.experimental.pallas import tpu as pltpu

F32 = jnp.float32
BF16 = jnp.bfloat16

EPS = 1e-6
NEG_INF = -1e30
HEAD_DIM = 128
WINDOW = 128
BLOCK = 128
N_BUCKETS = 32
MAX_DISTANCE = 128
FILTER_BANDS = 16

V7X_VMEM_LIMIT_BYTES = 56 * 1024 * 1024
LANES = 128
HALO = 16
GU_PAD = 8
CONVGLU_K_PIECES = 4
MATMUL_K_PIECES = 4
EPILOGUE_ROWS = 32
NORM_ROWS = 32


def _pick(n, pref, align):
    t = min(pref, n)
    t -= t % align
    while t >= align:
        if n % t == 0:
            return t
        t -= align
    return n


def _params(sem):
    return pltpu.CompilerParams(dimension_semantics=sem, vmem_limit_bytes=V7X_VMEM_LIMIT_BYTES)


def _rms_scale(x, g):
    ms = jnp.mean(x * x, axis=-1, keepdims=True)
    return x * lax.rsqrt(ms + EPS) * g


def _zero_token(x):
    bits = pltpu.bitcast(x, jnp.uint32)
    rows, cols = bits.shape
    acc = bits[:, 0:LANES]
    for j in range(1, cols // LANES):
        acc = acc | bits[:, j * LANES:(j + 1) * LANES]
    tok = acc[0:8]
    for i in range(1, rows // 8):
        tok = tok | acc[i * 8:(i + 1) * 8]
    return (tok >> 16) >> 16


def _order_after(ref, col, token):
    r = 8 * (4 // jnp.dtype(ref.dtype).itemsize)
    tile = ref[0:r, col:col + LANES]
    ref[0:r, col:col + LANES] = pltpu.bitcast(pltpu.bitcast(tile, jnp.uint32) | token, ref.dtype)


def _rms_rows_to(x_ref, g_ref, h_ref):
    rows = x_ref.shape[0]
    chunk = _pick(rows, NORM_ROWS, 8)
    g = g_ref[...]

    def body(c, carry):
        r = pl.ds(pl.multiple_of(c * chunk, chunk), chunk)
        h_ref[r, :] = _rms_scale(x_ref[r, :].astype(F32), g).astype(h_ref.dtype)
        return carry

    n_chunks = rows // chunk
    lax.fori_loop(0, n_chunks, body, 0, unroll=math.gcd(n_chunks, 8))


def _rmsnorm_kernel(x_ref, g_ref, o_ref):
    o_ref[...] = _rms_scale(x_ref[...].astype(F32), g_ref[...]).astype(o_ref.dtype)


def _rmsnorm(x, g, out_dtype):
    rows, d = x.shape
    tm = _pick(rows, 512, 8)
    return pl.pallas_call(
        _rmsnorm_kernel,
        grid=(rows // tm,),
        in_specs=[pl.BlockSpec((tm, d), lambda i: (i, 0)), pl.BlockSpec((1, d), lambda i: (0, 0))],
        out_specs=pl.BlockSpec((tm, d), lambda i: (i, 0)),
        out_shape=jax.ShapeDtypeStruct((rows, d), out_dtype),
        compiler_params=_params(("parallel",)),
        name="rmsnorm",
    )(x, g.reshape(1, d).astype(F32))


def _fused_matmul_kernel(*refs, n_pairs, n_extras, epilogue, norm, lag, n_j, n_tiles):
    s = pl.program_id(0)
    pos = 0
    a_refs, b_refs = [], []
    for p in range(n_pairs):
        a_refs.append(refs[pos])
        pos += 1
        if norm and p == 0:
            g_ref = refs[pos]
            pos += 1
        b_refs.append(refs[pos])
        pos += 1
    extra_refs = refs[pos:pos + n_extras]
    o_ref = refs[pos + n_extras]
    scratch = refs[pos + n_extras + 1:]
    t = jnp.minimum(s, n_tiles - 1)
    new_row_tile = (t % n_j == 0) & (s < n_tiles)
    if norm:
        h_ref, scratch = scratch[0], scratch[1:]

        @pl.when(new_row_tile)
        def _():
            _rms_rows_to(a_refs[0], g_ref, h_ref)

        a_refs[0] = h_ref

    if not lag:
        dots = [jnp.dot(a_refs[p][...], b_refs[p][...], preferred_element_type=F32) for p in range(n_pairs)]
        o_ref[...] = epilogue(dots, [r[...] for r in extra_refs]).astype(o_ref.dtype)
        return

    first_copy = 1 if norm else 0
    stage, scratch = scratch[:n_pairs - first_copy], scratch[n_pairs - first_copy:]

    @pl.when(new_row_tile)
    def _():
        for p in range(first_copy, n_pairs):
            stage[p - first_copy][...] = a_refs[p][...]

    for p in range(first_copy, n_pairs):
        a_refs[p] = stage[p - first_copy]
    buf_a, buf_b = scratch[:n_pairs], scratch[n_pairs:]
    tm = o_ref.shape[0]

    @pl.when(s == 0)
    def _():
        for r in buf_b:
            r[...] = jnp.zeros(r.shape, r.dtype)

    pieces = []
    for p in range(n_pairs):
        k = a_refs[p].shape[1]
        n_k = MATMUL_K_PIECES if k % (MATMUL_K_PIECES * LANES) == 0 and k // MATMUL_K_PIECES >= 512 else 1
        pieces += [(p, q, k // n_k) for q in range(n_k)]
    n_e = 1
    while n_e * 2 < len(pieces) and tm % (n_e * 2 * 8) == 0:
        n_e *= 2
    rc = tm // n_e

    def step(dst, src):
        for m, (p, q, kc) in enumerate(pieces):
            ksl = slice(q * kc, (q + 1) * kc)
            part = jnp.dot(a_refs[p][:, ksl], b_refs[p][ksl, :], preferred_element_type=F32)
            if q == 0:
                dst[p][...] = part
            else:
                dst[p][...] += part
            if m < n_e:
                token = None
                sub = min(rc, EPILOGUE_ROWS)
                for r0 in range(m * rc, (m + 1) * rc, sub):
                    rs = slice(r0, r0 + sub)
                    extras = [r[rs, :] if r.shape[0] == tm else r[...] for r in extra_refs]
                    out = epilogue([r[rs, :] for r in src], extras)
                    o_ref[rs, :] = out.astype(o_ref.dtype)
                    tk = _zero_token(out.astype(F32))
                    token = tk if token is None else token | tk
                if m + 1 < len(pieces):
                    pn, qn, kcn = pieces[m + 1]
                    _order_after(a_refs[pn], qn * kcn, token)

    @pl.when(s % 2 == 0)
    def _():
        step(buf_a, buf_b)

    @pl.when(s % 2 == 1)
    def _():
        step(buf_b, buf_a)


def _fused_matmul(pairs, extras, epilogue, out_dtype, n_cols, tm, tn, name, lag=False, norm_gain=None):
    rows = pairs[0][0].shape[0]
    assert rows % tm == 0 and n_cols % tn == 0
    n_i, n_j = rows // tm, n_cols // tn
    n_tiles = n_i * n_j
    norm = norm_gain is not None

    def dot_tile(s):
        return jnp.minimum(s, n_tiles - 1)

    def out_tile(s):
        return jnp.maximum(s - 1, 0) if lag else s

    args, in_specs, scratch = [], [], []
    for p, (a, b) in enumerate(pairs):
        k = a.shape[1]
        assert b.shape[0] == k
        if norm and p == 0:
            args += [a, norm_gain.reshape(1, k).astype(F32), b]
            in_specs += [pl.BlockSpec((tm, k), lambda s: (dot_tile(s) // n_j, 0), pipeline_mode=pl.Buffered(1)),
                         pl.BlockSpec((1, k), lambda s: (0, 0))]
            scratch.append(pltpu.VMEM((tm, k), BF16))
        else:
            args += [a, b]
            in_specs.append(pl.BlockSpec((tm, k), lambda s: (dot_tile(s) // n_j, 0)))
        in_specs.append(pl.BlockSpec((k, tn), lambda s: (0, dot_tile(s) % n_j)))
    for arr, kind, off in extras:
        assert off % tn == 0
        ob = off // tn
        args.append(arr)
        if kind == "tile":
            in_specs.append(pl.BlockSpec((tm, tn), lambda s, ob=ob: (out_tile(s) // n_j, out_tile(s) % n_j + ob)))
        else:
            in_specs.append(pl.BlockSpec((1, tn), lambda s, ob=ob: (0, out_tile(s) % n_j + ob)))
    if lag:
        scratch += [pltpu.VMEM((tm, a.shape[1]), a.dtype) for a, _ in pairs[(1 if norm else 0):]]
        scratch += [pltpu.VMEM((tm, tn), F32)] * (2 * len(pairs))
    return pl.pallas_call(
        functools.partial(_fused_matmul_kernel, n_pairs=len(pairs), n_extras=len(extras), epilogue=epilogue,
                          norm=norm, lag=lag, n_j=n_j, n_tiles=n_tiles),
        grid=(n_tiles + (1 if lag else 0),),
        in_specs=in_specs,
        out_specs=pl.BlockSpec((tm, tn), lambda s: (out_tile(s) // n_j, out_tile(s) % n_j)),
        out_shape=jax.ShapeDtypeStruct((rows, n_cols), out_dtype),
        scratch_shapes=scratch,
        compiler_params=_params(("arbitrary",)),
        name=name,
    )(*args)


def _ep_plain(dots, extras):
    return dots[0]


def _sigmoid(x):
    return 0.5 * (1.0 + jnp.tanh(0.5 * x))


def _ep_branch_merge(dots, extras):
    ga, gh = extras
    return _sigmoid(ga.astype(F32)) * dots[0] + _sigmoid(gh.astype(F32)) * dots[1]


def _ep_residual(dots, extras):
    return extras[0] + dots[0]


def _ep_ple(dots, extras):
    return extras[0] + _sigmoid(dots[0]) * dots[1]


def _bias_table_kernel(rb_ref, bkt_ref, o_ref):
    h = pl.program_id(0)
    bkt = bkt_ref[...]
    q = lax.broadcasted_iota(jnp.int32, bkt.shape, 0)
    s = lax.broadcasted_iota(jnp.int32, bkt.shape, 1)
    rel = s - BLOCK - q
    acc = jnp.zeros(bkt.shape, F32)
    for b in range(N_BUCKETS):
        acc = jnp.where(bkt == b, rb_ref[b, h], acc)
    o_ref[0] = jnp.where(jnp.abs(rel) <= WINDOW, acc, NEG_INF)


def _bias_table(rel_bias):
    n_heads = rel_bias.shape[1]
    half = N_BUCKETS // 2
    max_exact = half // 2
    qi = jnp.arange(BLOCK)[:, None]
    sj = jnp.arange(3 * BLOCK)[None, :]
    rel = sj - BLOCK - qi
    ret = jnp.where(rel > 0, half, 0)
    n = jnp.abs(rel)
    nf = jnp.maximum(n, 1).astype(F32)
    large = max_exact + (jnp.log(nf / max_exact) / math.log(MAX_DISTANCE / max_exact)
                         * (half - max_exact)).astype(jnp.int32)
    large = jnp.minimum(large, half - 1)
    bucket = (ret + jnp.where(n < max_exact, n, large)).astype(jnp.int32)
    return pl.pallas_call(
        _bias_table_kernel,
        grid=(n_heads,),
        in_specs=[pl.BlockSpec(memory_space=pltpu.SMEM),
                  pl.BlockSpec((BLOCK, 3 * BLOCK), lambda h: (0, 0))],
        out_specs=pl.BlockSpec((1, BLOCK, 3 * BLOCK), lambda h: (h, 0, 0)),
        out_shape=jax.ShapeDtypeStruct((n_heads, BLOCK, 3 * BLOCK), F32),
        compiler_params=_params(("arbitrary",)),
        name="attn_bias_table",
    )(rel_bias.astype(F32), bucket)


def _attn_kernel(sink_ref, q_ref, kp_ref, kc_ref, kn_ref, vp_ref, vc_ref, vn_ref, tab_ref, o_ref,
                 *, nb, n_kv, q_per_kv):
    n = pl.program_id(0) % nb
    col = lax.broadcasted_iota(jnp.int32, (1, 3 * BLOCK), 1)
    outside = ((col < BLOCK) & (n == 0)) | ((col >= 2 * BLOCK) & (n == nb - 1))
    scale = HEAD_DIM ** -0.5
    for kh in range(n_kv):
        ks = slice(kh * HEAD_DIM, (kh + 1) * HEAD_DIM)
        kband = jnp.concatenate([kp_ref[:, ks], kc_ref[:, ks], kn_ref[:, ks]], axis=0)
        vband = jnp.concatenate([vp_ref[:, ks], vc_ref[:, ks], vn_ref[:, ks]], axis=0)
        heads = range(kh * q_per_kv, (kh + 1) * q_per_kv)
        qh = jnp.concatenate([q_ref[:, h * HEAD_DIM:(h + 1) * HEAD_DIM] for h in heads], axis=0)
        s = lax.dot_general(qh, kband, (((1,), (1,)), ((), ())), preferred_element_type=F32) * scale
        s = s + tab_ref[kh * q_per_kv:(kh + 1) * q_per_kv].reshape(q_per_kv * BLOCK, 3 * BLOCK)
        s = jnp.where(outside, NEG_INF, s)
        sink = jnp.concatenate([jnp.full((BLOCK, 1), sink_ref[h], F32) for h in heads], axis=0)
        m = jnp.maximum(jnp.max(s, axis=-1, keepdims=True), sink)
        p = jnp.exp(s - m)
        denom = jnp.sum(p, axis=-1, keepdims=True) + jnp.exp(sink - m)
        o = jnp.dot(p.astype(BF16), vband, preferred_element_type=F32) / denom
        for g, h in enumerate(heads):
            o_ref[:, h * HEAD_DIM:(h + 1) * HEAD_DIM] = o[g * BLOCK:(g + 1) * BLOCK].astype(o_ref.dtype)


def _attention(z, tab, sink, seq_len, attn_w, kv_w):
    rows = z.shape[0]
    nb = seq_len // BLOCK
    nblk = rows // BLOCK
    n_kv = kv_w // HEAD_DIM
    q_per_kv = attn_w // kv_w
    assert attn_w % kv_w == 0
    kcol = attn_w // kv_w
    vcol = kcol + 1

    def prev(c):
        return lambda r: (jnp.maximum(r - 1, 0), c)

    def cur(c):
        return lambda r: (r, c)

    def nxt(c):
        return lambda r: (jnp.minimum(r + 1, nblk - 1), c)

    kv_spec = lambda f, c: pl.BlockSpec((BLOCK, kv_w), f(c))
    return pl.pallas_call(
        functools.partial(_attn_kernel, nb=nb, n_kv=n_kv, q_per_kv=q_per_kv),
        grid=(nblk,),
        in_specs=[pl.BlockSpec(memory_space=pltpu.SMEM),
                  pl.BlockSpec((BLOCK, attn_w), lambda r: (r, 0)),
                  kv_spec(prev, kcol), kv_spec(cur, kcol), kv_spec(nxt, kcol),
                  kv_spec(prev, vcol), kv_spec(cur, vcol), kv_spec(nxt, vcol),
                  pl.BlockSpec(tab.shape, lambda r: (0, 0, 0))],
        out_specs=pl.BlockSpec((BLOCK, attn_w), lambda r: (r, 0)),
        out_shape=jax.ShapeDtypeStruct((rows, attn_w), BF16),
        compiler_params=_params(("parallel",)),
        name="windowed_gqa",
    )(sink.astype(F32), z, z, z, z, z, z, z, tab)


def _conv3_full(x, w, b, rows, n_rows):
    xm = jnp.where(rows == 0, 0.0, pltpu.roll(x, 1, 0))
    xp = jnp.where(rows == n_rows - 1, 0.0, pltpu.roll(x, n_rows - 1, 0))
    return xm * w[0:1] + x * w[1:2] + xp * w[2:3] + b


def _hy_pre_kernel(v_ref, x1_ref, x0_ref, wv_ref, w1_ref, w0_ref, bv_ref, b1_ref, b0_ref, u_ref, x0c_ref):
    n_rows = v_ref.shape[1]
    rows = lax.broadcasted_iota(jnp.int32, (n_rows, 1), 0)
    hv = _conv3_full(v_ref[0].astype(F32), wv_ref[...], bv_ref[...], rows, n_rows)
    hx1 = _conv3_full(x1_ref[0].astype(F32), w1_ref[...], b1_ref[...], rows, n_rows)
    u_ref[0] = (hv * hx1).astype(u_ref.dtype)
    hx0 = _conv3_full(x0_ref[0].astype(F32), w0_ref[...], b0_ref[...], rows, n_rows)
    x0c_ref[0] = hx0.astype(x0c_ref.dtype)


def _hy_pre(z3, short_w, short_b, hy_off, hw):
    bsz, seq_len, _ = z3.shape
    tc = LANES
    assert hy_off % tc == 0 and hw % tc == 0
    ob, nc = hy_off // tc, hw // tc
    zspec = lambda part: pl.BlockSpec((1, seq_len, tc), lambda b, c: (b, 0, ob + part * nc + c))
    wspec = lambda part: pl.BlockSpec((3, tc), lambda b, c: (0, part * nc + c))
    bspec = lambda part: pl.BlockSpec((1, tc), lambda b, c: (0, part * nc + c))
    ospec = pl.BlockSpec((1, seq_len, tc), lambda b, c: (b, 0, c))
    sw = short_w.astype(F32)
    sb = short_b.reshape(1, -1).astype(F32)
    return pl.pallas_call(
        _hy_pre_kernel,
        grid=(bsz, nc),
        in_specs=[zspec(0), zspec(1), zspec(2), wspec(0), wspec(1), wspec(2), bspec(0), bspec(1), bspec(2)],
        out_specs=[ospec, ospec],
        out_shape=[jax.ShapeDtypeStruct((bsz, seq_len, hw), BF16)] * 2,
        compiler_params=_params(("parallel", "parallel")),
        name="hyena_short_conv",
    )(z3, z3, z3, sw, sw, sw, sb, sb, sb)


def _filter_mlp_kernel(zf_ref, w1_ref, b1_ref, f1_ref, w2_ref, b2_ref, f2_ref, w3f_ref, w3b_ref,
                       decf_ref, decb_ref, hs_ref, hd_ref):
    tl = zf_ref.shape[0]
    hp = lax.Precision.HIGHEST
    zf = zf_ref[...]
    h = jnp.sin(f1_ref[...] * (jnp.dot(zf, w1_ref[...], precision=hp, preferred_element_type=F32) + b1_ref[...]))
    h = jnp.sin(f2_ref[...] * (jnp.dot(h, w2_ref[...], precision=hp, preferred_element_type=F32) + b2_ref[...]))
    t = zf[:, 0:1]
    fwd = jnp.dot(h, w3f_ref[...], precision=hp, preferred_element_type=F32) * jnp.exp(-t * decf_ref[...])
    bwd = jnp.dot(h, w3b_ref[...], precision=hp, preferred_element_type=F32) * jnp.exp(-t * decb_ref[...])
    pos = pl.program_id(0) * tl + lax.broadcasted_iota(jnp.int32, (tl, 1), 0)
    bwd = jnp.where(pos == 0, 0.0, bwd)
    hs_ref[...] = (fwd + bwd).astype(hs_ref.dtype)
    hd_ref[...] = (fwd - bwd).astype(hd_ref.dtype)


def _pad2(a, rows, cols):
    return jnp.pad(a.astype(F32), ((0, rows - a.shape[0]), (0, cols - a.shape[1])))


def _filter_mlp(seq_len, w1, b1, f1, w2, b2, f2, w3, decay):
    hw = decay.shape[1]
    pos = jnp.arange(seq_len, dtype=F32)
    t = pos / (seq_len - 1)
    bands = jnp.linspace(1e-4, FILTER_BANDS - 1, FILTER_BANDS, dtype=F32)
    ang = (2.0 * math.pi / seq_len) * pos[:, None] * bands[None, :]
    zf = jnp.concatenate([t[:, None], jnp.cos(ang), -jnp.sin(ang)], axis=-1)
    emb, hid = w1.shape
    assert emb <= LANES and hid <= LANES
    zf = _pad2(zf, seq_len, LANES)
    row = lambda v: _pad2(v.reshape(1, -1), 1, LANES)
    w3p = _pad2(w3, LANES, 2 * hw)
    dec = decay.reshape(1, 2 * hw).astype(F32)
    tl = _pick(seq_len, 512, 8)
    tc = _pick(hw, 512, LANES)
    nc = hw // tc
    small = lambda shape: pl.BlockSpec(shape, lambda i, c: (0, 0))
    return pl.pallas_call(
        _filter_mlp_kernel,
        grid=(seq_len // tl, nc),
        in_specs=[pl.BlockSpec((tl, LANES), lambda i, c: (i, 0)),
                  small((LANES, LANES)), small((1, LANES)), small((1, LANES)),
                  small((LANES, LANES)), small((1, LANES)), small((1, LANES)),
                  pl.BlockSpec((LANES, tc), lambda i, c: (0, c)),
                  pl.BlockSpec((LANES, tc), lambda i, c: (0, nc + c)),
                  pl.BlockSpec((1, tc), lambda i, c: (0, c)),
                  pl.BlockSpec((1, tc), lambda i, c: (0, nc + c))],
        out_specs=[pl.BlockSpec((tl, tc), lambda i, c: (i, c))] * 2,
        out_shape=[jax.ShapeDtypeStruct((seq_len, hw), BF16)] * 2,
        compiler_params=_params(("parallel", "parallel")),
        name="hyena_filter_mlp",
    )(zf, _pad2(w1, LANES, LANES), row(b1), row(f1), _pad2(w2, LANES, LANES), row(b2), row(f2),
      w3p, w3p, dec, dec)


def _dft_mats_kernel(cb_ref, sb_ref, ck_ref, sk_ref, fc_ref, fs_ref, fst_ref):
    tk, n_cols = cb_ref.shape
    cb, sb = cb_ref[...], sb_ref[...]
    ck, sk = ck_ref[0], sk_ref[0]
    cs = ck * cb - sk * sb
    sn = sk * cb + ck * sb
    k = pl.program_id(0) * tk + lax.broadcasted_iota(jnp.int32, (tk, 1), 0)
    n = lax.broadcasted_iota(jnp.int32, (1, n_cols), 1)
    alt_n = jnp.where(n % 2 == 1, -1.0, 1.0)
    alt_k = jnp.where(k % 2 == 1, -1.0, 1.0)
    fc_ref[...] = cs.astype(fc_ref.dtype)
    fs_ref[...] = jnp.where(k == 0, alt_n, sn).astype(fs_ref.dtype)
    fst_ref[...] = jnp.where(n == 0, alt_k, sn).astype(fst_ref.dtype)


def _dft_mats(seq_len):
    n2 = 2 * seq_len
    tk = _pick(seq_len, 256, HALO)
    n_i = seq_len // tk
    n = jnp.arange(seq_len, dtype=jnp.int32)[None, :]

    def tables(k):
        ang = ((k * n) % n2).astype(F32) * (2.0 * math.pi / n2)
        return jnp.cos(ang), jnp.sin(ang)

    cb, sb = tables(jnp.arange(tk, dtype=jnp.int32)[:, None])
    ck, sk = tables(jnp.arange(n_i, dtype=jnp.int32)[:, None] * tk)
    base = pl.BlockSpec((tk, seq_len), lambda i: (0, 0))
    rowt = pl.BlockSpec((1, 1, seq_len), lambda i: (i, 0, 0))
    ospec = pl.BlockSpec((tk, seq_len), lambda i: (i, 0))
    return pl.pallas_call(
        _dft_mats_kernel,
        grid=(n_i,),
        in_specs=[base, base, rowt, rowt],
        out_specs=[ospec] * 3,
        out_shape=[jax.ShapeDtypeStruct((seq_len, seq_len), BF16)] * 3,
        compiler_params=_params(("parallel",)),
        name="dft_matrices",
    )(cb, sb, ck.reshape(n_i, 1, seq_len), sk.reshape(n_i, 1, seq_len))


def _filter_spectrum_kernel(fc_ref, fs_ref, hs_ref, hd_ref, a_ref, d_ref, *, seq_len):
    tm = fc_ref.shape[0]
    hs = hs_ref[...]
    row0 = (pl.program_id(0) * tm + lax.broadcasted_iota(jnp.int32, (tm, 1), 0)) == 0
    scale = jnp.where(row0, 1.0 / (2 * seq_len), 2.0 / (2 * seq_len))
    a_ref[...] = scale * jnp.dot(fc_ref[...], hs, preferred_element_type=F32)
    fs = fs_ref[...]
    d = jnp.where(row0, jnp.dot(fs, hs, preferred_element_type=F32),
                  jnp.dot(fs, hd_ref[...], preferred_element_type=F32))
    d_ref[...] = scale * d


def _filter_spectrum(fc, fs, hs, hd):
    seq_len, hw = hs.shape
    tm = _pick(seq_len, 512, HALO)
    tn = _pick(hw, 512, LANES)
    fspec = pl.BlockSpec((tm, seq_len), lambda i, j: (i, 0))
    hspec = pl.BlockSpec((seq_len, tn), lambda i, j: (0, j))
    ospec = pl.BlockSpec((tm, tn), lambda i, j: (i, j))
    return pl.pallas_call(
        functools.partial(_filter_spectrum_kernel, seq_len=seq_len),
        grid=(seq_len // tm, hw // tn),
        in_specs=[fspec, fspec, hspec, hspec],
        out_specs=[ospec, ospec],
        out_shape=[jax.ShapeDtypeStruct((seq_len, hw), F32)] * 2,
        compiler_params=_params(("parallel", "parallel")),
        name="hyena_filter_spectrum",
    )(fc, fs, hs, hd)


def _dft_fwd_kernel(fc_ref, fs_ref, u_ref, a_ref, d_ref, w1_ref, w2_ref):
    tm = fc_ref.shape[0]
    u = u_ref[0]
    p = jnp.dot(fc_ref[...], u, preferred_element_type=F32)
    q = jnp.dot(fs_ref[...], u, preferred_element_type=F32)
    a = a_ref[...]
    d = d_ref[...]
    row0 = (pl.program_id(0) * tm + lax.broadcasted_iota(jnp.int32, (tm, 1), 0)) == 0
    qd = q * d
    w1_ref[0] = (p * a - jnp.where(row0, 0.0, qd)).astype(w1_ref.dtype)
    w2_ref[0] = jnp.where(row0, qd, p * d + q * a).astype(w2_ref.dtype)


def _dft_fwd(fc, fs, u, a, d):
    bsz, seq_len, hw = u.shape
    tm = _pick(seq_len, 512, HALO)
    tn = _pick(hw, 512, LANES)
    fspec = pl.BlockSpec((tm, seq_len), lambda i, j, b: (i, 0))
    kspec = pl.BlockSpec((tm, tn), lambda i, j, b: (i, j))
    ospec = pl.BlockSpec((1, tm, tn), lambda i, j, b: (b, i, j))
    return pl.pallas_call(
        _dft_fwd_kernel,
        grid=(seq_len // tm, hw // tn, bsz),
        in_specs=[fspec, fspec, pl.BlockSpec((1, seq_len, tn), lambda i, j, b: (b, 0, j)), kspec, kspec],
        out_specs=[ospec, ospec],
        out_shape=[jax.ShapeDtypeStruct((bsz, seq_len, hw), BF16)] * 2,
        compiler_params=_params(("parallel", "parallel", "parallel")),
        name="hyena_dft_forward",
    )(fc, fs, u, a, d)


def _dft_inv_kernel(fc_ref, fst_ref, w1_ref, w2_ref, u_ref, x0c_ref, skip_ref, o_ref):
    y = (jnp.dot(fc_ref[...], w1_ref[0], preferred_element_type=F32)
         + jnp.dot(fst_ref[...], w2_ref[0], preferred_element_type=F32))
    y = y + u_ref[0].astype(F32) * skip_ref[...]
    o_ref[0] = (y * x0c_ref[0].astype(F32)).astype(o_ref.dtype)


def _dft_inv(fc, fst, w1, w2, u, x0c, skip):
    bsz, seq_len, hw = u.shape
    tm = _pick(seq_len, 512, HALO)
    tn = _pick(hw, 512, LANES)
    fspec = pl.BlockSpec((tm, seq_len), lambda i, j, b: (i, 0))
    wspec = pl.BlockSpec((1, seq_len, tn), lambda i, j, b: (b, 0, j))
    tspec = pl.BlockSpec((1, tm, tn), lambda i, j, b: (b, i, j))
    return pl.pallas_call(
        _dft_inv_kernel,
        grid=(seq_len // tm, hw // tn, bsz),
        in_specs=[fspec, fspec, wspec, wspec, tspec, tspec, pl.BlockSpec((1, tn), lambda i, j, b: (0, j))],
        out_specs=tspec,
        out_shape=jax.ShapeDtypeStruct((bsz, seq_len, hw), BF16),
        compiler_params=_params(("parallel", "parallel", "parallel")),
        name="hyena_dft_inverse",
    )(fc, fst, w1, w2, u, x0c, skip.reshape(1, hw).astype(F32))


def _ffn_up_kernel(x_ref, xp_ref, xn_ref, g_ref, wg_ref, wv_ref, cw_ref, cb_ref, o_ref,
                   h_ref, halo_ref, gu_a, val_a, gu_b, val_b, *, seq_len, n_j, n_tiles):
    s = pl.program_id(0)
    tm, d = x_ref.shape
    t = jnp.minimum(s, n_tiles - 1)

    @pl.when((t % n_j == 0) & (s < n_tiles))
    def _():
        g = g_ref[...]
        _rms_rows_to(x_ref, g_ref, h_ref)
        halo_ref[0:HALO] = _rms_scale(xp_ref[...], g).astype(halo_ref.dtype)
        halo_ref[HALO:2 * HALO] = _rms_scale(xn_ref[...], g).astype(halo_ref.dtype)

    @pl.when(s == 0)
    def _():
        for r in (gu_b, val_b):
            r[...] = jnp.zeros(r.shape, r.dtype)

    n_k = CONVGLU_K_PIECES if d % (CONVGLU_K_PIECES * LANES) == 0 else 1
    kc = d // n_k
    n_pieces = 2 * n_k
    rc = tm // n_pieces
    body_rows = pl.ds(GU_PAD, tm)

    def step(dst, src):
        gu_d, val_d = dst
        gu_s, val_s = src
        cw = cw_ref[...]
        cb = cb_ref[...]
        for p in range(n_pieces):
            q = p % n_k
            ksl = slice(q * kc, (q + 1) * kc)
            w_ref, acc_ref, acc_rows = (wg_ref, gu_d, body_rows) if p < n_k else (wv_ref, val_d, slice(None))
            part = jnp.dot(h_ref[:, ksl], w_ref[ksl, :], preferred_element_type=F32)
            if q == 0:
                acc_ref[acc_rows, :] = part
            else:
                acc_ref[acc_rows, :] += part
            token = None
            sub = min(rc, EPILOGUE_ROWS)
            for r0 in range(p * rc, (p + 1) * rc, sub):
                conv = (gu_s[pl.ds(GU_PAD - 1 + r0, sub), :] * cw[0:1] + gu_s[pl.ds(GU_PAD + r0, sub), :] * cw[1:2]
                        + gu_s[pl.ds(GU_PAD + 1 + r0, sub), :] * cw[2:3] + cb)
                gelu = 0.5 * conv * (1.0 + lax.erf(conv * math.sqrt(0.5)))
                act = gelu * val_s[r0:r0 + sub, :]
                o_ref[r0:r0 + sub, :] = act.astype(o_ref.dtype)
                tk = _zero_token(act)
                token = tk if token is None else token | tk
            if p + 1 < n_pieces:
                _order_after(h_ref, ((p + 1) % n_k) * kc, token)
            else:
                _order_after(halo_ref, 0, token)
        hg = jnp.dot(halo_ref[...], wg_ref[...], preferred_element_type=F32)
        row_start = (t // n_j) * tm
        gu_d[GU_PAD - 1:GU_PAD, :] = jnp.where(row_start % seq_len == 0, 0.0, hg[HALO - 1:HALO])
        gu_d[GU_PAD + tm:GU_PAD + tm + 1, :] = jnp.where((row_start + tm) % seq_len == 0, 0.0, hg[HALO:HALO + 1])

    @pl.when(s % 2 == 0)
    def _():
        step((gu_a, val_a), (gu_b, val_b))

    @pl.when(s % 2 == 1)
    def _():
        step((gu_b, val_b), (gu_a, val_a))


def _ffn_up(x, gain, w_up, conv_w, conv_b, seq_len):
    rows, d = x.shape
    d_ff = w_up.shape[1] // 2
    tm = _pick(seq_len, 1024, HALO)
    tn = _pick(d_ff, 256, LANES)
    n_i, n_j = rows // tm, d_ff // tn
    n_tiles = n_i * n_j
    hb = tm // HALO
    n_halo = rows // HALO
    dot_i = lambda s: jnp.minimum(s, n_tiles - 1) // n_j
    dot_j = lambda s: jnp.minimum(s, n_tiles - 1) % n_j
    out_i = lambda s: jnp.maximum(s - 1, 0) // n_j
    out_j = lambda s: jnp.maximum(s - 1, 0) % n_j
    return pl.pallas_call(
        functools.partial(_ffn_up_kernel, seq_len=seq_len, n_j=n_j, n_tiles=n_tiles),
        grid=(n_tiles + 1,),
        in_specs=[pl.BlockSpec((tm, d), lambda s: (dot_i(s), 0), pipeline_mode=pl.Buffered(1)),
                  pl.BlockSpec((HALO, d), lambda s: (jnp.maximum(dot_i(s) * hb - 1, 0), 0)),
                  pl.BlockSpec((HALO, d), lambda s: (jnp.minimum((dot_i(s) + 1) * hb, n_halo - 1), 0)),
                  pl.BlockSpec((1, d), lambda s: (0, 0)),
                  pl.BlockSpec((d, tn), lambda s: (0, dot_j(s))),
                  pl.BlockSpec((d, tn), lambda s: (0, n_j + dot_j(s))),
                  pl.BlockSpec((3, tn), lambda s: (0, out_j(s))),
                  pl.BlockSpec((1, tn), lambda s: (0, out_j(s)))],
        out_specs=pl.BlockSpec((tm, tn), lambda s: (out_i(s), out_j(s))),
        out_shape=jax.ShapeDtypeStruct((rows, d_ff), BF16),
        scratch_shapes=[pltpu.VMEM((tm, d), BF16), pltpu.VMEM((2 * HALO, d), BF16)]
                       + [pltpu.VMEM((tm + 2 * GU_PAD, tn), F32), pltpu.VMEM((tm, tn), F32)] * 2,
        compiler_params=_params(("arbitrary",)),
        name="convglu_up",
    )(x, x, x, gain.reshape(1, d).astype(F32), w_up, w_up, conv_w.astype(F32),
      conv_b.reshape(1, d_ff).astype(F32))


def _prep_layer(w):
    names = ("w_in", "w_attn_o", "w_hyena_o", "w_out", "w_up", "w_down", "w_ple_gate", "w_ple")
    return {k: w[k].astype(BF16) for k in names}


def _encoder_layer(x, p, tab, w, wp, seq_len):
    rows, d_model = x.shape
    bsz = rows // seq_len
    attn_w = w["w_attn_o"].shape[0]
    hw = w["w_hyena_o"].shape[0]
    in_cols = w["w_in"].shape[1]
    kv_w = (in_cols - attn_w - 3 * hw - 2 * d_model) // 2
    hy_off = attn_w + 2 * kv_w
    gate_off = hy_off + 3 * hw
    tm = _pick(seq_len, 1024, HALO)

    z = _fused_matmul([(x, wp["w_in"])], [], _ep_plain, BF16, in_cols, tm, _pick(in_cols, 1024, LANES),
                      "in_proj", norm_gain=w["g_mix"])
    attn = _attention(z, tab, w["attn_sink"], seq_len, attn_w, kv_w)

    u, x0c = _hy_pre(z.reshape(bsz, seq_len, in_cols), w["hy_short_w"], w["hy_short_b"], hy_off, hw)
    hs, hd = _filter_mlp(seq_len, w["hy_filt_w1"], w["hy_filt_b1"], w["hy_filt_f1"], w["hy_filt_w2"],
                         w["hy_filt_b2"], w["hy_filt_f2"], w["hy_filt_w3"], w["hy_decay"])
    fc, fs, fst = _dft_mats(seq_len)
    ka, kd = _filter_spectrum(fc, fs, hs, hd)
    w1, w2 = _dft_fwd(fc, fs, u, ka, kd)
    hy_out = _dft_inv(fc, fst, w1, w2, u, x0c, w["hy_skip"]).reshape(rows, hw)

    tn = _pick(math.gcd(d_model, gate_off), 512, LANES)
    merged = _fused_matmul([(attn, wp["w_attn_o"]), (hy_out, wp["w_hyena_o"])],
                           [(z, "tile", gate_off), (z, "tile", gate_off + d_model)],
                           _ep_branch_merge, BF16, d_model, tm, tn, "branch_merge")
    tn = _pick(d_model, 512, LANES)
    x = _fused_matmul([(merged, wp["w_out"])], [(x, "tile", 0)], _ep_residual, F32, d_model, tm, tn, "out_proj")

    act = _ffn_up(x, w["g_ffn"], wp["w_up"], w["ffn_conv_w"], w["ffn_conv_b"], seq_len)
    x = _fused_matmul([(act, wp["w_down"])], [(x, "tile", 0)], _ep_residual, F32, d_model,
                      _pick(seq_len, 512, HALO), _pick(d_model, 256, LANES), "ffn_down")

    x = _fused_matmul([(x, wp["w_ple_gate"]), (p.astype(BF16), wp["w_ple"])], [(x, "tile", 0)],
                      _ep_ple, F32, d_model, tm, tn, "ple_gate", lag=True, norm_gain=w["g_ple"])
    return x


_LAYER_WEIGHT_NAMES = ("g_mix", "w_in", "attn_sink", "hy_short_w", "hy_short_b", "hy_filt_w1", "hy_filt_b1",
                       "hy_filt_f1", "hy_filt_w2", "hy_filt_b2", "hy_filt_f2", "hy_filt_w3", "hy_decay",
                       "hy_skip", "w_attn_o", "w_hyena_o", "w_out", "g_ffn", "w_up", "ffn_conv_w", "ffn_conv_b",
                       "w_down", "g_ple", "w_ple_gate", "w_ple")


def _trunk(x, p, tab, layers, preps, g_final):
    bsz, seq_len, d_model = x.shape
    xf = x.reshape(bsz * seq_len, d_model)
    for l, (w, wp) in enumerate(zip(layers, preps)):
        xf = _encoder_layer(xf, p[l].reshape(bsz * seq_len, -1), tab, w, wp, seq_len)
    return _rmsnorm(xf, g_final, x.dtype).reshape(bsz, seq_len, d_model)


def kernel(x_prompt, x_sample, p_prompt, p_sample, rel_bias, g_mix, w_in, attn_sink, hy_short_w, hy_short_b, hy_filt_w1, hy_filt_b1, hy_filt_f1, hy_filt_w2, hy_filt_b2, hy_filt_f2, hy_filt_w3, hy_decay, hy_skip, w_attn_o, w_hyena_o, w_out, g_ffn, w_up, ffn_conv_w, ffn_conv_b, w_down, g_ple, w_ple_gate, w_ple, g_final):
    stacked = (g_mix, w_in, attn_sink, hy_short_w, hy_short_b, hy_filt_w1, hy_filt_b1, hy_filt_f1, hy_filt_w2,
               hy_filt_b2, hy_filt_f2, hy_filt_w3, hy_decay, hy_skip, w_attn_o, w_hyena_o, w_out, g_ffn, w_up,
               ffn_conv_w, ffn_conv_b, w_down, g_ple, w_ple_gate, w_ple)
    depth = g_mix.shape[0]
    layers = [dict(zip(_LAYER_WEIGHT_NAMES, [a[l] for a in stacked])) for l in range(depth)]
    preps = [_prep_layer(w) for w in layers]
    tab = _bias_table(rel_bias)
    y_prompt = _trunk(x_prompt, p_prompt, tab, layers, preps, g_final)
    y_sample = _trunk(x_sample, p_sample, tab, layers, preps, g_final)
    return (y_prompt, y_sample)
```

```python
import functools
import math

import jax
import jax.numpy as jnp
from jax import lax
from jax.experimental import pallas as pl
from jax.experimental.pallas import tpu as pltpu

F32 = jnp.float32
BF16 = jnp.bfloat16

EPS = 1e-6
NEG_INF = -1e30
HEAD_DIM = 128
WINDOW = 128
BLOCK = 128
N_BUCKETS = 32
MAX_DISTANCE = 128
FILTER_BANDS = 16

V7X_VMEM_LIMIT_BYTES = 56 * 1024 * 1024
LANES = 128
HALO = 16
GU_PAD = 8
CONVGLU_K_PIECES = 1
EPILOGUE_ROWS = 32
NORM_ROWS = 32
ATTN_ROWS = 32
ATTN_BLOCKS_PER_STEP = 4


def _pick(n, pref, align):
    t = min(pref, n)
    t -= t % align
    while t >= align:
        if n % t == 0:
            return t
        t -= align
    return n


def _params(sem):
    return pltpu.CompilerParams(dimension_semantics=sem, vmem_limit_bytes=V7X_VMEM_LIMIT_BYTES)


def _rms_scale(x, g):
    ms = jnp.mean(x * x, axis=-1, keepdims=True)
    return x * lax.rsqrt(ms + EPS) * g


def _zero_token(x):
    bits = pltpu.bitcast(x, jnp.uint32)
    rows, cols = bits.shape
    acc = bits[:, 0:LANES]
    for j in range(1, cols // LANES):
        acc = acc | bits[:, j * LANES:(j + 1) * LANES]
    tok = acc[0:8]
    for i in range(1, rows // 8):
        tok = tok | acc[i * 8:(i + 1) * 8]
    return (tok >> 16) >> 16


def _order_after(ref, col, token):
    r = 8 * (4 // jnp.dtype(ref.dtype).itemsize)
    tile = ref[0:r, col:col + LANES]
    ref[0:r, col:col + LANES] = pltpu.bitcast(pltpu.bitcast(tile, jnp.uint32) | token, ref.dtype)


def _rms_rows_to(x_ref, g_ref, h_ref):
    rows = x_ref.shape[0]
    chunk = _pick(rows, NORM_ROWS, 8)
    g = g_ref[...]

    def body(c, carry):
        r = pl.ds(pl.multiple_of(c * chunk, chunk), chunk)
        h_ref[r, :] = _rms_scale(x_ref[r, :].astype(F32), g).astype(h_ref.dtype)
        return carry

    n_chunks = rows // chunk
    lax.fori_loop(0, n_chunks, body, 0, unroll=math.gcd(n_chunks, 8))


def _rmsnorm_kernel(x_ref, g_ref, o_ref):
    o_ref[...] = _rms_scale(x_ref[...].astype(F32), g_ref[...]).astype(o_ref.dtype)


def _rmsnorm(x, g, out_dtype):
    rows, d = x.shape
    tm = _pick(rows, 512, 8)
    return pl.pallas_call(
        _rmsnorm_kernel,
        grid=(rows // tm,),
        in_specs=[pl.BlockSpec((tm, d), lambda i: (i, 0)), pl.BlockSpec((1, d), lambda i: (0, 0))],
        out_specs=pl.BlockSpec((tm, d), lambda i: (i, 0)),
        out_shape=jax.ShapeDtypeStruct((rows, d), out_dtype),
        compiler_params=_params(("parallel",)),
        name="rmsnorm",
    )(x, g.reshape(1, d).astype(F32))


def _fused_matmul_kernel(*refs, n_pairs, n_extras, epilogue, norm, n_j):
    s = pl.program_id(0)
    pos = 0
    a_refs, b_refs = [], []
    for p in range(n_pairs):
        a_refs.append(refs[pos])
        pos += 1
        if norm and p == 0:
            g_ref = refs[pos]
            pos += 1
        b_refs.append(refs[pos])
        pos += 1
    extra_refs = refs[pos:pos + n_extras]
    o_ref = refs[pos + n_extras]
    if norm:
        h_ref = refs[pos + n_extras + 1]

        @pl.when(s % n_j == 0)
        def _():
            _rms_rows_to(a_refs[0], g_ref, h_ref)

        a_refs[0] = h_ref

    dots = [jnp.dot(a_refs[p][...], b_refs[p][...], preferred_element_type=F32) for p in range(n_pairs)]
    tm = o_ref.shape[0]
    sub = _pick(tm, EPILOGUE_ROWS, 8)
    for r0 in range(0, tm, sub):
        rs = slice(r0, r0 + sub)
        extras = [r[rs, :] if r.shape[0] == tm else r[...] for r in extra_refs]
        o_ref[rs, :] = epilogue([d[rs] for d in dots], extras).astype(o_ref.dtype)


def _fused_matmul(pairs, extras, epilogue, out_dtype, n_cols, tm, tn, name, norm_gain=None):
    rows = pairs[0][0].shape[0]
    assert rows % tm == 0 and n_cols % tn == 0
    n_i, n_j = rows // tm, n_cols // tn
    norm = norm_gain is not None
    args, in_specs, scratch = [], [], []
    for p, (a, b) in enumerate(pairs):
        k = a.shape[1]
        assert b.shape[0] == k
        if norm and p == 0:
            args += [a, norm_gain.reshape(1, k).astype(F32), b]
            in_specs += [pl.BlockSpec((tm, k), lambda s: (s // n_j, 0), pipeline_mode=pl.Buffered(1)),
                         pl.BlockSpec((1, k), lambda s: (0, 0))]
            scratch.append(pltpu.VMEM((tm, k), BF16))
        else:
            args += [a, b]
            in_specs.append(pl.BlockSpec((tm, k), lambda s: (s // n_j, 0)))
        in_specs.append(pl.BlockSpec((k, tn), lambda s: (0, s % n_j)))
    for arr, kind, off in extras:
        assert off % tn == 0
        ob = off // tn
        args.append(arr)
        if kind == "tile":
            in_specs.append(pl.BlockSpec((tm, tn), lambda s, ob=ob: (s // n_j, s % n_j + ob)))
        else:
            in_specs.append(pl.BlockSpec((1, tn), lambda s, ob=ob: (0, s % n_j + ob)))
    return pl.pallas_call(
        functools.partial(_fused_matmul_kernel, n_pairs=len(pairs), n_extras=len(extras), epilogue=epilogue,
                          norm=norm, n_j=n_j),
        grid=(n_i * n_j,),
        in_specs=in_specs,
        out_specs=pl.BlockSpec((tm, tn), lambda s: (s // n_j, s % n_j)),
        out_shape=jax.ShapeDtypeStruct((rows, n_cols), out_dtype),
        scratch_shapes=scratch,
        compiler_params=_params(("arbitrary",)),
        name=name,
    )(*args)


def _ep_plain(dots, extras):
    return dots[0]


def _sigmoid(x):
    return 0.5 * (1.0 + jnp.tanh(0.5 * x))


def _ep_branch_merge(dots, extras):
    ga, gh = extras
    return _sigmoid(ga.astype(F32)) * dots[0] + _sigmoid(gh.astype(F32)) * dots[1]


def _ep_residual(dots, extras):
    return extras[0] + dots[0]


def _ep_ple(dots, extras):
    return extras[0] + _sigmoid(dots[0]) * dots[1]


def _bias_table_kernel(rb_ref, bkt_ref, o_ref):
    h = pl.program_id(0)
    bkt = bkt_ref[...]
    q = lax.broadcasted_iota(jnp.int32, bkt.shape, 0)
    s = lax.broadcasted_iota(jnp.int32, bkt.shape, 1)
    rel = s - BLOCK - q
    acc = jnp.zeros(bkt.shape, F32)
    for b in range(N_BUCKETS):
        acc = jnp.where(bkt == b, rb_ref[b, h], acc)
    o_ref[0] = jnp.where(jnp.abs(rel) <= WINDOW, acc, NEG_INF)


def _bias_table(rel_bias):
    n_heads = rel_bias.shape[1]
    half = N_BUCKETS // 2
    max_exact = half // 2
    qi = jnp.arange(BLOCK)[:, None]
    sj = jnp.arange(3 * BLOCK)[None, :]
    rel = sj - BLOCK - qi
    ret = jnp.where(rel > 0, half, 0)
    n = jnp.abs(rel)
    nf = jnp.maximum(n, 1).astype(F32)
    large = max_exact + (jnp.log(nf / max_exact) / math.log(MAX_DISTANCE / max_exact)
                         * (half - max_exact)).astype(jnp.int32)
    large = jnp.minimum(large, half - 1)
    bucket = (ret + jnp.where(n < max_exact, n, large)).astype(jnp.int32)
    return pl.pallas_call(
        _bias_table_kernel,
        grid=(n_heads,),
        in_specs=[pl.BlockSpec(memory_space=pltpu.SMEM),
                  pl.BlockSpec((BLOCK, 3 * BLOCK), lambda h: (0, 0))],
        out_specs=pl.BlockSpec((1, BLOCK, 3 * BLOCK), lambda h: (h, 0, 0)),
        out_shape=jax.ShapeDtypeStruct((n_heads, BLOCK, 3 * BLOCK), F32),
        compiler_params=_params(("arbitrary",)),
        name="attn_bias_table",
    )(rel_bias.astype(F32), bucket)


def _attn_kernel(sink_ref, q_ref, kp_ref, km_ref, kn_ref, vp_ref, vm_ref, vn_ref, tab_ref, o_ref, p_ref,
                 *, nb, nq, n_kv, q_per_kv):
    first = (pl.program_id(0) * nq) % nb
    col = lax.broadcasted_iota(jnp.int32, (1, 3 * BLOCK), 1)
    scale = HEAD_DIM ** -0.5
    n_rows = q_per_kv * BLOCK

    def band(prev_ref, main_ref, next_ref, j, ks):
        parts = []
        for b in (j - 1, j, j + 1):
            if b < 0:
                parts.append(prev_ref[:, ks])
            elif b >= nq:
                parts.append(next_ref[:, ks])
            else:
                parts.append(main_ref[b * BLOCK:(b + 1) * BLOCK, ks])
        return jnp.concatenate(parts, axis=0)

    for j in range(nq):
        n = first + j
        outside = ((col < BLOCK) & (n == 0)) | ((col >= 2 * BLOCK) & (n == nb - 1))
        qrows = slice(j * BLOCK, (j + 1) * BLOCK)
        for kh in range(n_kv):
            ks = slice(kh * HEAD_DIM, (kh + 1) * HEAD_DIM)
            kband = band(kp_ref, km_ref, kn_ref, j, ks)
            vband = band(vp_ref, vm_ref, vn_ref, j, ks)
            heads = list(range(kh * q_per_kv, (kh + 1) * q_per_kv))
            qh = jnp.concatenate([q_ref[qrows, h * HEAD_DIM:(h + 1) * HEAD_DIM] for h in heads], axis=0)
            s_all = lax.dot_general(qh, kband, (((1,), (1,)), ((), ())), preferred_element_type=F32)
            inv = []
            for c in range(0, n_rows, ATTN_ROWS):
                h = heads[c // BLOCK]
                s = s_all[c:c + ATTN_ROWS] * scale + tab_ref[h, c % BLOCK:c % BLOCK + ATTN_ROWS, :]
                s = jnp.where(outside, NEG_INF, s)
                sink = sink_ref[h]
                m = jnp.maximum(jnp.max(s, axis=-1, keepdims=True), sink)
                p = jnp.exp(s - m)
                denom = jnp.sum(p, axis=-1, keepdims=True) + jnp.exp(sink - m)
                p_ref[c:c + ATTN_ROWS, :] = p.astype(p_ref.dtype)
                inv.append(denom)
            o = jnp.dot(p_ref[...], vband, preferred_element_type=F32) / jnp.concatenate(inv, axis=0)
            for g, h in enumerate(heads):
                o_ref[qrows, h * HEAD_DIM:(h + 1) * HEAD_DIM] = o[g * BLOCK:(g + 1) * BLOCK].astype(o_ref.dtype)


def _attention(z, tab, sink, seq_len, attn_w, kv_w):
    rows = z.shape[0]
    nb = seq_len // BLOCK
    nq = math.gcd(nb, ATTN_BLOCKS_PER_STEP)
    nblk = rows // BLOCK
    n_kv = kv_w // HEAD_DIM
    q_per_kv = attn_w // kv_w
    assert attn_w % kv_w == 0
    kcol = attn_w // kv_w
    vcol = kcol + 1

    def prev(c):
        return lambda r: (jnp.maximum(r * nq - 1, 0), c)

    def main(c):
        return lambda r: (r, c)

    def nxt(c):
        return lambda r: (jnp.minimum((r + 1) * nq, nblk - 1), c)

    side_spec = lambda f, c: pl.BlockSpec((BLOCK, kv_w), f(c))
    main_spec = lambda c: pl.BlockSpec((nq * BLOCK, kv_w), main(c))
    return pl.pallas_call(
        functools.partial(_attn_kernel, nb=nb, nq=nq, n_kv=n_kv, q_per_kv=q_per_kv),
        grid=(nblk // nq,),
        in_specs=[pl.BlockSpec(memory_space=pltpu.SMEM),
                  pl.BlockSpec((nq * BLOCK, attn_w), lambda r: (r, 0)),
                  side_spec(prev, kcol), main_spec(kcol), side_spec(nxt, kcol),
                  side_spec(prev, vcol), main_spec(vcol), side_spec(nxt, vcol),
                  pl.BlockSpec(tab.shape, lambda r: (0, 0, 0))],
        out_specs=pl.BlockSpec((nq * BLOCK, attn_w), lambda r: (r, 0)),
        out_shape=jax.ShapeDtypeStruct((rows, attn_w), BF16),
        scratch_shapes=[pltpu.VMEM((q_per_kv * BLOCK, 3 * BLOCK), BF16)],
        compiler_params=_params(("parallel",)),
        name="windowed_gqa",
    )(sink.astype(F32), z, z, z, z, z, z, z, tab)


def _conv3_full(x, w, b, rows, n_rows):
    xm = jnp.where(rows == 0, 0.0, pltpu.roll(x, 1, 0))
    xp = jnp.where(rows == n_rows - 1, 0.0, pltpu.roll(x, n_rows - 1, 0))
    return xm * w[0:1] + x * w[1:2] + xp * w[2:3] + b


def _hy_pre_kernel(v_ref, x1_ref, x0_ref, wv_ref, w1_ref, w0_ref, bv_ref, b1_ref, b0_ref, u_ref, x0c_ref):
    n_rows = v_ref.shape[1]
    rows = lax.broadcasted_iota(jnp.int32, (n_rows, 1), 0)
    hv = _conv3_full(v_ref[0].astype(F32), wv_ref[...], bv_ref[...], rows, n_rows)
    hx1 = _conv3_full(x1_ref[0].astype(F32), w1_ref[...], b1_ref[...], rows, n_rows)
    u_ref[0] = (hv * hx1).astype(u_ref.dtype)
    hx0 = _conv3_full(x0_ref[0].astype(F32), w0_ref[...], b0_ref[...], rows, n_rows)
    x0c_ref[0] = hx0.astype(x0c_ref.dtype)


def _hy_pre(z3, short_w, short_b, hy_off, hw):
    bsz, seq_len, _ = z3.shape
    tc = LANES
    assert hy_off % tc == 0 and hw % tc == 0
    ob, nc = hy_off // tc, hw // tc
    zspec = lambda part: pl.BlockSpec((1, seq_len, tc), lambda b, c: (b, 0, ob + part * nc + c))
    wspec = lambda part: pl.BlockSpec((3, tc), lambda b, c: (0, part * nc + c))
    bspec = lambda part: pl.BlockSpec((1, tc), lambda b, c: (0, part * nc + c))
    ospec = pl.BlockSpec((1, seq_len, tc), lambda b, c: (b, 0, c))
    sw = short_w.astype(F32)
    sb = short_b.reshape(1, -1).astype(F32)
    return pl.pallas_call(
        _hy_pre_kernel,
        grid=(bsz, nc),
        in_specs=[zspec(0), zspec(1), zspec(2), wspec(0), wspec(1), wspec(2), bspec(0), bspec(1), bspec(2)],
        out_specs=[ospec, ospec],
        out_shape=[jax.ShapeDtypeStruct((bsz, seq_len, hw), BF16)] * 2,
        compiler_params=_params(("parallel", "parallel")),
        name="hyena_short_conv",
    )(z3, z3, z3, sw, sw, sw, sb, sb, sb)


def _filter_mlp_kernel(zf_ref, w1_ref, b1_ref, f1_ref, w2_ref, b2_ref, f2_ref, w3f_ref, w3b_ref,
                       decf_ref, decb_ref, hs_ref, hd_ref):
    tl = zf_ref.shape[0]
    hp = lax.Precision.HIGHEST
    zf = zf_ref[...]
    h = jnp.sin(f1_ref[...] * (jnp.dot(zf, w1_ref[...], precision=hp, preferred_element_type=F32) + b1_ref[...]))
    h = jnp.sin(f2_ref[...] * (jnp.dot(h, w2_ref[...], precision=hp, preferred_element_type=F32) + b2_ref[...]))
    t = zf[:, 0:1]
    fwd = jnp.dot(h, w3f_ref[...], precision=hp, preferred_element_type=F32) * jnp.exp(-t * decf_ref[...])
    bwd = jnp.dot(h, w3b_ref[...], precision=hp, preferred_element_type=F32) * jnp.exp(-t * decb_ref[...])
    pos = pl.program_id(0) * tl + lax.broadcasted_iota(jnp.int32, (tl, 1), 0)
    bwd = jnp.where(pos == 0, 0.0, bwd)
    hs_ref[...] = (fwd + bwd).astype(hs_ref.dtype)
    hd_ref[...] = (fwd - bwd).astype(hd_ref.dtype)


def _pad2(a, rows, cols):
    return jnp.pad(a.astype(F32), ((0, rows - a.shape[0]), (0, cols - a.shape[1])))


def _filter_mlp(seq_len, w1, b1, f1, w2, b2, f2, w3, decay):
    hw = decay.shape[1]
    pos = jnp.arange(seq_len, dtype=F32)
    t = pos / (seq_len - 1)
    bands = jnp.linspace(1e-4, FILTER_BANDS - 1, FILTER_BANDS, dtype=F32)
    ang = (2.0 * math.pi / seq_len) * pos[:, None] * bands[None, :]
    zf = jnp.concatenate([t[:, None], jnp.cos(ang), -jnp.sin(ang)], axis=-1)
    emb, hid = w1.shape
    assert emb <= LANES and hid <= LANES
    zf = _pad2(zf, seq_len, LANES)
    row = lambda v: _pad2(v.reshape(1, -1), 1, LANES)
    w3p = _pad2(w3, LANES, 2 * hw)
    dec = decay.reshape(1, 2 * hw).astype(F32)
    tl = _pick(seq_len, 512, 8)
    tc = _pick(hw, 512, LANES)
    nc = hw // tc
    small = lambda shape: pl.BlockSpec(shape, lambda i, c: (0, 0))
    return pl.pallas_call(
        _filter_mlp_kernel,
        grid=(seq_len // tl, nc),
        in_specs=[pl.BlockSpec((tl, LANES), lambda i, c: (i, 0)),
                  small((LANES, LANES)), small((1, LANES)), small((1, LANES)),
                  small((LANES, LANES)), small((1, LANES)), small((1, LANES)),
                  pl.BlockSpec((LANES, tc), lambda i, c: (0, c)),
                  pl.BlockSpec((LANES, tc), lambda i, c: (0, nc + c)),
                  pl.BlockSpec((1, tc), lambda i, c: (0, c)),
                  pl.BlockSpec((1, tc), lambda i, c: (0, nc + c))],
        out_specs=[pl.BlockSpec((tl, tc), lambda i, c: (i, c))] * 2,
        out_shape=[jax.ShapeDtypeStruct((seq_len, hw), BF16)] * 2,
        compiler_params=_params(("parallel", "parallel")),
        name="hyena_filter_mlp",
    )(zf, _pad2(w1, LANES, LANES), row(b1), row(f1), _pad2(w2, LANES, LANES), row(b2), row(f2),
      w3p, w3p, dec, dec)


def _dft_mats_kernel(cb_ref, sb_ref, ck_ref, sk_ref, fc_ref, fs_ref, fst_ref):
    tk, n_cols = cb_ref.shape
    cb, sb = cb_ref[...], sb_ref[...]
    ck, sk = ck_ref[0], sk_ref[0]
    cs = ck * cb - sk * sb
    sn = sk * cb + ck * sb
    k = pl.program_id(0) * tk + lax.broadcasted_iota(jnp.int32, (tk, 1), 0)
    n = lax.broadcasted_iota(jnp.int32, (1, n_cols), 1)
    alt_n = jnp.where(n % 2 == 1, -1.0, 1.0)
    alt_k = jnp.where(k % 2 == 1, -1.0, 1.0)
    fc_ref[...] = cs.astype(fc_ref.dtype)
    fs_ref[...] = jnp.where(k == 0, alt_n, sn).astype(fs_ref.dtype)
    fst_ref[...] = jnp.where(n == 0, alt_k, sn).astype(fst_ref.dtype)


def _dft_mats(seq_len):
    n2 = 2 * seq_len
    tk = _pick(seq_len, 256, HALO)
    n_i = seq_len // tk
    n = jnp.arange(seq_len, dtype=jnp.int32)[None, :]

    def tables(k):
        ang = ((k * n) % n2).astype(F32) * (2.0 * math.pi / n2)
        return jnp.cos(ang), jnp.sin(ang)

    cb, sb = tables(jnp.arange(tk, dtype=jnp.int32)[:, None])
    ck, sk = tables(jnp.arange(n_i, dtype=jnp.int32)[:, None] * tk)
    base = pl.BlockSpec((tk, seq_len), lambda i: (0, 0))
    rowt = pl.BlockSpec((1, 1, seq_len), lambda i: (i, 0, 0))
    ospec = pl.BlockSpec((tk, seq_len), lambda i: (i, 0))
    return pl.pallas_call(
        _dft_mats_kernel,
        grid=(n_i,),
        in_specs=[base, base, rowt, rowt],
        out_specs=[ospec] * 3,
        out_shape=[jax.ShapeDtypeStruct((seq_len, seq_len), BF16)] * 3,
        compiler_params=_params(("parallel",)),
        name="dft_matrices",
    )(cb, sb, ck.reshape(n_i, 1, seq_len), sk.reshape(n_i, 1, seq_len))


def _filter_spectrum_kernel(fc_ref, fs_ref, hs_ref, hd_ref, a_ref, d_ref, *, seq_len):
    tm = fc_ref.shape[0]
    hs = hs_ref[...]
    row0 = (pl.program_id(0) * tm + lax.broadcasted_iota(jnp.int32, (tm, 1), 0)) == 0
    scale = jnp.where(row0, 1.0 / (2 * seq_len), 2.0 / (2 * seq_len))
    a_ref[...] = scale * jnp.dot(fc_ref[...], hs, preferred_element_type=F32)
    fs = fs_ref[...]
    d = jnp.where(row0, jnp.dot(fs, hs, preferred_element_type=F32),
                  jnp.dot(fs, hd_ref[...], preferred_element_type=F32))
    d_ref[...] = scale * d


def _filter_spectrum(fc, fs, hs, hd):
    seq_len, hw = hs.shape
    tm = _pick(seq_len, 512, HALO)
    tn = _pick(hw, 512, LANES)
    fspec = pl.BlockSpec((tm, seq_len), lambda i, j: (i, 0))
    hspec = pl.BlockSpec((seq_len, tn), lambda i, j: (0, j))
    ospec = pl.BlockSpec((tm, tn), lambda i, j: (i, j))
    return pl.pallas_call(
        functools.partial(_filter_spectrum_kernel, seq_len=seq_len),
        grid=(seq_len // tm, hw // tn),
        in_specs=[fspec, fspec, hspec, hspec],
        out_specs=[ospec, ospec],
        out_shape=[jax.ShapeDtypeStruct((seq_len, hw), F32)] * 2,
        compiler_params=_params(("parallel", "parallel")),
        name="hyena_filter_spectrum",
    )(fc, fs, hs, hd)


def _dft_fwd_kernel(fc_ref, fs_ref, u_ref, a_ref, d_ref, w1_ref, w2_ref):
    tm = fc_ref.shape[0]
    u = u_ref[0]
    p_all = jnp.dot(fc_ref[...], u, preferred_element_type=F32)
    q_all = jnp.dot(fs_ref[...], u, preferred_element_type=F32)
    sub = _pick(tm, EPILOGUE_ROWS, 8)
    for r0 in range(0, tm, sub):
        rs = slice(r0, r0 + sub)
        p, q, a, d = p_all[rs], q_all[rs], a_ref[rs, :], d_ref[rs, :]
        row0 = (pl.program_id(0) * tm + r0 + lax.broadcasted_iota(jnp.int32, (sub, 1), 0)) == 0
        qd = q * d
        w1_ref[0, rs, :] = (p * a - jnp.where(row0, 0.0, qd)).astype(w1_ref.dtype)
        w2_ref[0, rs, :] = jnp.where(row0, qd, p * d + q * a).astype(w2_ref.dtype)


def _dft_fwd(fc, fs, u, a, d):
    bsz, seq_len, hw = u.shape
    tm = _pick(seq_len, 512, HALO)
    tn = _pick(hw, 512, LANES)
    fspec = pl.BlockSpec((tm, seq_len), lambda i, j, b: (i, 0))
    kspec = pl.BlockSpec((tm, tn), lambda i, j, b: (i, j))
    ospec = pl.BlockSpec((1, tm, tn), lambda i, j, b: (b, i, j))
    return pl.pallas_call(
        _dft_fwd_kernel,
        grid=(seq_len // tm, hw // tn, bsz),
        in_specs=[fspec, fspec, pl.BlockSpec((1, seq_len, tn), lambda i, j, b: (b, 0, j)), kspec, kspec],
        out_specs=[ospec, ospec],
        out_shape=[jax.ShapeDtypeStruct((bsz, seq_len, hw), BF16)] * 2,
        compiler_params=_params(("parallel", "parallel", "parallel")),
        name="hyena_dft_forward",
    )(fc, fs, u, a, d)


def _dft_inv_kernel(fc_ref, fst_ref, w1_ref, w2_ref, u_ref, x0c_ref, skip_ref, o_ref):
    y1 = jnp.dot(fc_ref[...], w1_ref[0], preferred_element_type=F32)
    y2 = jnp.dot(fst_ref[...], w2_ref[0], preferred_element_type=F32)
    tm = o_ref.shape[1]
    sub = _pick(tm, EPILOGUE_ROWS, 8)
    skip = skip_ref[...]
    for r0 in range(0, tm, sub):
        rs = slice(r0, r0 + sub)
        y = y1[rs] + y2[rs] + u_ref[0, rs, :].astype(F32) * skip
        o_ref[0, rs, :] = (y * x0c_ref[0, rs, :].astype(F32)).astype(o_ref.dtype)


def _dft_inv(fc, fst, w1, w2, u, x0c, skip):
    bsz, seq_len, hw = u.shape
    tm = _pick(seq_len, 512, HALO)
    tn = _pick(hw, 512, LANES)
    fspec = pl.BlockSpec((tm, seq_len), lambda i, j, b: (i, 0))
    wspec = pl.BlockSpec((1, seq_len, tn), lambda i, j, b: (b, 0, j))
    tspec = pl.BlockSpec((1, tm, tn), lambda i, j, b: (b, i, j))
    return pl.pallas_call(
        _dft_inv_kernel,
        grid=(seq_len // tm, hw // tn, bsz),
        in_specs=[fspec, fspec, wspec, wspec, tspec, tspec, pl.BlockSpec((1, tn), lambda i, j, b: (0, j))],
        out_specs=tspec,
        out_shape=jax.ShapeDtypeStruct((bsz, seq_len, hw), BF16),
        compiler_params=_params(("parallel", "parallel", "parallel")),
        name="hyena_dft_inverse",
    )(fc, fst, w1, w2, u, x0c, skip.reshape(1, hw).astype(F32))


def _ffn_up_kernel(x_ref, xp_ref, xn_ref, g_ref, wg_ref, wv_ref, cw_ref, cb_ref, o_ref,
                   h_ref, halo_ref, gu_a, val_a, gu_b, val_b, *, seq_len, n_j, n_tiles):
    s = pl.program_id(0)
    tm, d = x_ref.shape
    t = jnp.minimum(s, n_tiles - 1)

    @pl.when((t % n_j == 0) & (s < n_tiles))
    def _():
        g = g_ref[...]
        _rms_rows_to(x_ref, g_ref, h_ref)
        halo_ref[0:HALO] = _rms_scale(xp_ref[...], g).astype(halo_ref.dtype)
        halo_ref[HALO:2 * HALO] = _rms_scale(xn_ref[...], g).astype(halo_ref.dtype)

    @pl.when(s == 0)
    def _():
        for r in (gu_b, val_b):
            r[...] = jnp.zeros(r.shape, r.dtype)

    n_k = CONVGLU_K_PIECES if d % (CONVGLU_K_PIECES * LANES) == 0 else 1
    kc = d // n_k
    n_pieces = 2 * n_k
    rc = tm // n_pieces
    body_rows = pl.ds(GU_PAD, tm)

    def step(dst, src):
        gu_d, val_d = dst
        gu_s, val_s = src
        cw = cw_ref[...]
        cb = cb_ref[...]
        for p in range(n_pieces):
            q = p % n_k
            ksl = slice(q * kc, (q + 1) * kc)
            w_ref, acc_ref, acc_rows = (wg_ref, gu_d, body_rows) if p < n_k else (wv_ref, val_d, slice(None))
            part = jnp.dot(h_ref[:, ksl], w_ref[ksl, :], preferred_element_type=F32)
            if q == 0:
                acc_ref[acc_rows, :] = part
            else:
                acc_ref[acc_rows, :] += part
            token = None
            sub = min(rc, EPILOGUE_ROWS)
            for r0 in range(p * rc, (p + 1) * rc, sub):
                conv = (gu_s[pl.ds(GU_PAD - 1 + r0, sub), :] * cw[0:1] + gu_s[pl.ds(GU_PAD + r0, sub), :] * cw[1:2]
                        + gu_s[pl.ds(GU_PAD + 1 + r0, sub), :] * cw[2:3] + cb)
                gelu = 0.5 * conv * (1.0 + lax.erf(conv * math.sqrt(0.5)))
                act = gelu * val_s[r0:r0 + sub, :]
                o_ref[r0:r0 + sub, :] = act.astype(o_ref.dtype)
                tk = _zero_token(act)
                token = tk if token is None else token | tk
            if p + 1 < n_pieces:
                _order_after(h_ref, ((p + 1) % n_k) * kc, token)
            else:
                _order_after(halo_ref, 0, token)
        hg = jnp.dot(halo_ref[...], wg_ref[...], preferred_element_type=F32)
        row_start = (t // n_j) * tm
        gu_d[GU_PAD - 1:GU_PAD, :] = jnp.where(row_start % seq_len == 0, 0.0, hg[HALO - 1:HALO])
        gu_d[GU_PAD + tm:GU_PAD + tm + 1, :] = jnp.where((row_start + tm) % seq_len == 0, 0.0, hg[HALO:HALO + 1])

    @pl.when(s % 2 == 0)
    def _():
        step((gu_a, val_a), (gu_b, val_b))

    @pl.when(s % 2 == 1)
    def _():
        step((gu_b, val_b), (gu_a, val_a))


def _ffn_up(x, gain, w_up, conv_w, conv_b, seq_len):
    rows, d = x.shape
    d_ff = w_up.shape[1] // 2
    tm = _pick(seq_len, 1024, HALO)
    tn = _pick(d_ff, 256, LANES)
    n_i, n_j = rows // tm, d_ff // tn
    n_tiles = n_i * n_j
    hb = tm // HALO
    n_halo = rows // HALO
    dot_i = lambda s: jnp.minimum(s, n_tiles - 1) // n_j
    dot_j = lambda s: jnp.minimum(s, n_tiles - 1) % n_j
    out_i = lambda s: jnp.maximum(s - 1, 0) // n_j
    out_j = lambda s: jnp.maximum(s - 1, 0) % n_j
    return pl.pallas_call(
        functools.partial(_ffn_up_kernel, seq_len=seq_len, n_j=n_j, n_tiles=n_tiles),
        grid=(n_tiles + 1,),
        in_specs=[pl.BlockSpec((tm, d), lambda s: (dot_i(s), 0), pipeline_mode=pl.Buffered(1)),
                  pl.BlockSpec((HALO, d), lambda s: (jnp.maximum(dot_i(s) * hb - 1, 0), 0)),
                  pl.BlockSpec((HALO, d), lambda s: (jnp.minimum((dot_i(s) + 1) * hb, n_halo - 1), 0)),
                  pl.BlockSpec((1, d), lambda s: (0, 0)),
                  pl.BlockSpec((d, tn), lambda s: (0, dot_j(s))),
                  pl.BlockSpec((d, tn), lambda s: (0, n_j + dot_j(s))),
                  pl.BlockSpec((3, tn), lambda s: (0, out_j(s))),
                  pl.BlockSpec((1, tn), lambda s: (0, out_j(s)))],
        out_specs=pl.BlockSpec((tm, tn), lambda s: (out_i(s), out_j(s))),
        out_shape=jax.ShapeDtypeStruct((rows, d_ff), BF16),
        scratch_shapes=[pltpu.VMEM((tm, d), BF16), pltpu.VMEM((2 * HALO, d), BF16)]
                       + [pltpu.VMEM((tm + 2 * GU_PAD, tn), F32), pltpu.VMEM((tm, tn), F32)] * 2,
        compiler_params=_params(("arbitrary",)),
        name="convglu_up",
    )(x, x, x, gain.reshape(1, d).astype(F32), w_up, w_up, conv_w.astype(F32),
      conv_b.reshape(1, d_ff).astype(F32))


def _prep_layer(w):
    names = ("w_in", "w_attn_o", "w_hyena_o", "w_out", "w_up", "w_down", "w_ple_gate", "w_ple")
    return {k: w[k].astype(BF16) for k in names}


def _encoder_layer(x, p, tab, w, wp, seq_len):
    rows, d_model = x.shape
    bsz = rows // seq_len
    attn_w = w["w_attn_o"].shape[0]
    hw = w["w_hyena_o"].shape[0]
    in_cols = w["w_in"].shape[1]
    kv_w = (in_cols - attn_w - 3 * hw - 2 * d_model) // 2
    hy_off = attn_w + 2 * kv_w
    gate_off = hy_off + 3 * hw
    tm = _pick(seq_len, 1024, HALO)

    z = _fused_matmul([(x, wp["w_in"])], [], _ep_plain, BF16, in_cols, tm, _pick(in_cols, 1024, LANES),
                      "in_proj", norm_gain=w["g_mix"])
    attn = _attention(z, tab, w["attn_sink"], seq_len, attn_w, kv_w)

    u, x0c = _hy_pre(z.reshape(bsz, seq_len, in_cols), w["hy_short_w"], w["hy_short_b"], hy_off, hw)
    hs, hd = _filter_mlp(seq_len, w["hy_filt_w1"], w["hy_filt_b1"], w["hy_filt_f1"], w["hy_filt_w2"],
                         w["hy_filt_b2"], w["hy_filt_f2"], w["hy_filt_w3"], w["hy_decay"])
    fc, fs, fst = _dft_mats(seq_len)
    ka, kd = _filter_spectrum(fc, fs, hs, hd)
    w1, w2 = _dft_fwd(fc, fs, u, ka, kd)
    hy_out = _dft_inv(fc, fst, w1, w2, u, x0c, w["hy_skip"]).reshape(rows, hw)

    tn = _pick(math.gcd(d_model, gate_off), 512, LANES)
    merged = _fused_matmul([(attn, wp["w_attn_o"]), (hy_out, wp["w_hyena_o"])],
                           [(z, "tile", gate_off), (z, "tile", gate_off + d_model)],
                           _ep_branch_merge, BF16, d_model, tm, tn, "branch_merge")
    tn = _pick(d_model, 512, LANES)
    x = _fused_matmul([(merged, wp["w_out"])], [(x, "tile", 0)], _ep_residual, F32, d_model, tm, tn, "out_proj")

    act = _ffn_up(x, w["g_ffn"], wp["w_up"], w["ffn_conv_w"], w["ffn_conv_b"], seq_len)
    x = _fused_matmul([(act, wp["w_down"])], [(x, "tile", 0)], _ep_residual, F32, d_model,
                      _pick(seq_len, 512, HALO), _pick(d_model, 256, LANES), "ffn_down")

    x = _fused_matmul([(x, wp["w_ple_gate"]), (p.astype(BF16), wp["w_ple"])], [(x, "tile", 0)],
                      _ep_ple, F32, d_model, tm, tn, "ple_gate", norm_gain=w["g_ple"])
    return x


_LAYER_WEIGHT_NAMES = ("g_mix", "w_in", "attn_sink", "hy_short_w", "hy_short_b", "hy_filt_w1", "hy_filt_b1",
                       "hy_filt_f1", "hy_filt_w2", "hy_filt_b2", "hy_filt_f2", "hy_filt_w3", "hy_decay",
                       "hy_skip", "w_attn_o", "w_hyena_o", "w_out", "g_ffn", "w_up", "ffn_conv_w", "ffn_conv_b",
                       "w_down", "g_ple", "w_ple_gate", "w_ple")


def _trunk(x, p, tab, layers, preps, g_final):
    bsz, seq_len, d_model = x.shape
    xf = x.reshape(bsz * seq_len, d_model)
    for l, (w, wp) in enumerate(zip(layers, preps)):
        xf = _encoder_layer(xf, p[l].reshape(bsz * seq_len, -1), tab, w, wp, seq_len)
    return _rmsnorm(xf, g_final, x.dtype).reshape(bsz, seq_len, d_model)


def kernel(x_prompt, x_sample, p_prompt, p_sample, rel_bias, g_mix, w_in, attn_sink, hy_short_w, hy_short_b, hy_filt_w1, hy_filt_b1, hy_filt_f1, hy_filt_w2, hy_filt_b2, hy_filt_f2, hy_filt_w3, hy_decay, hy_skip, w_attn_o, w_hyena_o, w_out, g_ffn, w_up, ffn_conv_w, ffn_conv_b, w_down, g_ple, w_ple_gate, w_ple, g_final):
    stacked = (g_mix, w_in, attn_sink, hy_short_w, hy_short_b, hy_filt_w1, hy_filt_b1, hy_filt_f1, hy_filt_w2,
               hy_filt_b2, hy_filt_f2, hy_filt_w3, hy_decay, hy_skip, w_attn_o, w_hyena_o, w_out, g_ffn, w_up,
               ffn_conv_w, ffn_conv_b, w_down, g_ple, w_ple_gate, w_ple)
    depth = g_mix.shape[0]
    layers = [dict(zip(_LAYER_WEIGHT_NAMES, [a[l] for a in stacked])) for l in range(depth)]
    preps = [_prep_layer(w) for w in layers]
    tab = _bias_table(rel_bias)
    y_prompt = _trunk(x_prompt, p_prompt, tab, layers, preps, g_final)
    y_sample = _trunk(x_sample, p_sample, tab, layers, preps, g_final)
    return (y_prompt, y_sample)
```

```python
import functools
import math

import jax
import jax.numpy as jnp
from jax import lax
from jax.experimental import pallas as pl
from jax.experimental.pallas import tpu as pltpu

F32 = jnp.float32
BF16 = jnp.bfloat16

EPS = 1e-6
NEG_INF = -1e30
HEAD_DIM = 128
WINDOW = 128
BLOCK = 128
N_BUCKETS = 32
MAX_DISTANCE = 128
FILTER_BANDS = 16

V7X_VMEM_LIMIT_BYTES = 56 * 1024 * 1024
LANES = 128
HALO = 16
EPILOGUE_ROWS = 32
NORM_ROWS = 32
ATTN_ROWS = 32
ATTN_BLOCKS_PER_STEP = 4


def _pick(n, pref, align):
    t = min(pref, n)
    t -= t % align
    while t >= align:
        if n % t == 0:
            return t
        t -= align
    return n


def _params(sem):
    return pltpu.CompilerParams(dimension_semantics=sem, vmem_limit_bytes=V7X_VMEM_LIMIT_BYTES)


def _rms_scale(x, g):
    ms = jnp.mean(x * x, axis=-1, keepdims=True)
    return x * lax.rsqrt(ms + EPS) * g


def _rms_rows_to(x_ref, g_ref, h_ref):
    rows = x_ref.shape[0]
    chunk = _pick(rows, NORM_ROWS, 8)
    g = g_ref[...]

    def body(c, carry):
        r = pl.ds(pl.multiple_of(c * chunk, chunk), chunk)
        h_ref[r, :] = _rms_scale(x_ref[r, :].astype(F32), g).astype(h_ref.dtype)
        return carry

    n_chunks = rows // chunk
    lax.fori_loop(0, n_chunks, body, 0, unroll=math.gcd(n_chunks, 8))


def _rmsnorm_kernel(x_ref, g_ref, o_ref):
    o_ref[...] = _rms_scale(x_ref[...].astype(F32), g_ref[...]).astype(o_ref.dtype)


def _rmsnorm(x, g, out_dtype):
    rows, d = x.shape
    tm = _pick(rows, 512, 8)
    return pl.pallas_call(
        _rmsnorm_kernel,
        grid=(rows // tm,),
        in_specs=[pl.BlockSpec((tm, d), lambda i: (i, 0)), pl.BlockSpec((1, d), lambda i: (0, 0))],
        out_specs=pl.BlockSpec((tm, d), lambda i: (i, 0)),
        out_shape=jax.ShapeDtypeStruct((rows, d), out_dtype),
        compiler_params=_params(("parallel",)),
        name="rmsnorm",
    )(x, g.reshape(1, d).astype(F32))


def _fused_matmul_kernel(*refs, n_pairs, n_extras, epilogue, norm, n_j):
    s = pl.program_id(0)
    pos = 0
    a_refs, b_refs = [], []
    for p in range(n_pairs):
        a_refs.append(refs[pos])
        pos += 1
        if norm and p == 0:
            g_ref = refs[pos]
            pos += 1
        b_refs.append(refs[pos])
        pos += 1
    extra_refs = refs[pos:pos + n_extras]
    o_ref = refs[pos + n_extras]
    if norm:
        h_ref = refs[pos + n_extras + 1]

        @pl.when(s % n_j == 0)
        def _():
            _rms_rows_to(a_refs[0], g_ref, h_ref)

        a_refs[0] = h_ref

    dots = [jnp.dot(a_refs[p][...], b_refs[p][...], preferred_element_type=F32) for p in range(n_pairs)]
    tm = o_ref.shape[0]
    sub = _pick(tm, EPILOGUE_ROWS, 8)
    for r0 in range(0, tm, sub):
        rs = slice(r0, r0 + sub)
        extras = [r[rs, :] if r.shape[0] == tm else r[...] for r in extra_refs]
        o_ref[rs, :] = epilogue([d[rs] for d in dots], extras).astype(o_ref.dtype)


def _fused_matmul(pairs, extras, epilogue, out_dtype, n_cols, tm, tn, name, norm_gain=None):
    rows = pairs[0][0].shape[0]
    assert rows % tm == 0 and n_cols % tn == 0
    n_i, n_j = rows // tm, n_cols // tn
    norm = norm_gain is not None
    args, in_specs, scratch = [], [], []
    for p, (a, b) in enumerate(pairs):
        k = a.shape[1]
        assert b.shape[0] == k
        if norm and p == 0:
            args += [a, norm_gain.reshape(1, k).astype(F32), b]
            in_specs += [pl.BlockSpec((tm, k), lambda s: (s // n_j, 0), pipeline_mode=pl.Buffered(1)),
                         pl.BlockSpec((1, k), lambda s: (0, 0))]
            scratch.append(pltpu.VMEM((tm, k), BF16))
        else:
            args += [a, b]
            in_specs.append(pl.BlockSpec((tm, k), lambda s: (s // n_j, 0)))
        in_specs.append(pl.BlockSpec((k, tn), lambda s: (0, s % n_j)))
    for arr, kind, off in extras:
        assert off % tn == 0
        ob = off // tn
        args.append(arr)
        if kind == "tile":
            in_specs.append(pl.BlockSpec((tm, tn), lambda s, ob=ob: (s // n_j, s % n_j + ob)))
        else:
            in_specs.append(pl.BlockSpec((1, tn), lambda s, ob=ob: (0, s % n_j + ob)))
    return pl.pallas_call(
        functools.partial(_fused_matmul_kernel, n_pairs=len(pairs), n_extras=len(extras), epilogue=epilogue,
                          norm=norm, n_j=n_j),
        grid=(n_i * n_j,),
        in_specs=in_specs,
        out_specs=pl.BlockSpec((tm, tn), lambda s: (s // n_j, s % n_j)),
        out_shape=jax.ShapeDtypeStruct((rows, n_cols), out_dtype),
        scratch_shapes=scratch,
        compiler_params=_params(("arbitrary",)),
        name=name,
    )(*args)


def _ep_plain(dots, extras):
    return dots[0]


def _sigmoid(x):
    return 0.5 * (1.0 + jnp.tanh(0.5 * x))


def _ep_branch_merge(dots, extras):
    ga, gh = extras
    return _sigmoid(ga.astype(F32)) * dots[0] + _sigmoid(gh.astype(F32)) * dots[1]


def _ep_residual(dots, extras):
    return extras[0] + dots[0]


def _ep_ple(dots, extras):
    return extras[0] + _sigmoid(dots[0]) * dots[1]


def _bias_table_kernel(rb_ref, bkt_ref, o_ref):
    h = pl.program_id(0)
    bkt = bkt_ref[...]
    q = lax.broadcasted_iota(jnp.int32, bkt.shape, 0)
    s = lax.broadcasted_iota(jnp.int32, bkt.shape, 1)
    rel = s - BLOCK - q
    acc = jnp.zeros(bkt.shape, F32)
    for b in range(N_BUCKETS):
        acc = jnp.where(bkt == b, rb_ref[b, h], acc)
    o_ref[0] = jnp.where(jnp.abs(rel) <= WINDOW, acc, NEG_INF)


def _bias_table(rel_bias):
    n_heads = rel_bias.shape[1]
    half = N_BUCKETS // 2
    max_exact = half // 2
    qi = jnp.arange(BLOCK)[:, None]
    sj = jnp.arange(3 * BLOCK)[None, :]
    rel = sj - BLOCK - qi
    ret = jnp.where(rel > 0, half, 0)
    n = jnp.abs(rel)
    nf = jnp.maximum(n, 1).astype(F32)
    large = max_exact + (jnp.log(nf / max_exact) / math.log(MAX_DISTANCE / max_exact)
                         * (half - max_exact)).astype(jnp.int32)
    large = jnp.minimum(large, half - 1)
    bucket = (ret + jnp.where(n < max_exact, n, large)).astype(jnp.int32)
    return pl.pallas_call(
        _bias_table_kernel,
        grid=(n_heads,),
        in_specs=[pl.BlockSpec(memory_space=pltpu.SMEM),
                  pl.BlockSpec((BLOCK, 3 * BLOCK), lambda h: (0, 0))],
        out_specs=pl.BlockSpec((1, BLOCK, 3 * BLOCK), lambda h: (h, 0, 0)),
        out_shape=jax.ShapeDtypeStruct((n_heads, BLOCK, 3 * BLOCK), F32),
        compiler_params=_params(("arbitrary",)),
        name="attn_bias_table",
    )(rel_bias.astype(F32), bucket)


def _attn_kernel(sink_ref, q_ref, kp_ref, km_ref, kn_ref, vp_ref, vm_ref, vn_ref, tab_ref, o_ref, p_ref,
                 *, nb, nq, n_kv, q_per_kv):
    first = (pl.program_id(0) * nq) % nb
    col = lax.broadcasted_iota(jnp.int32, (1, 3 * BLOCK), 1)
    scale = HEAD_DIM ** -0.5
    n_rows = q_per_kv * BLOCK

    def band(prev_ref, main_ref, next_ref, j, ks):
        parts = []
        for b in (j - 1, j, j + 1):
            if b < 0:
                parts.append(prev_ref[:, ks])
            elif b >= nq:
                parts.append(next_ref[:, ks])
            else:
                parts.append(main_ref[b * BLOCK:(b + 1) * BLOCK, ks])
        return jnp.concatenate(parts, axis=0)

    for j in range(nq):
        n = first + j
        outside = ((col < BLOCK) & (n == 0)) | ((col >= 2 * BLOCK) & (n == nb - 1))
        qrows = slice(j * BLOCK, (j + 1) * BLOCK)
        for kh in range(n_kv):
            ks = slice(kh * HEAD_DIM, (kh + 1) * HEAD_DIM)
            kband = band(kp_ref, km_ref, kn_ref, j, ks)
            vband = band(vp_ref, vm_ref, vn_ref, j, ks)
            heads = list(range(kh * q_per_kv, (kh + 1) * q_per_kv))
            qh = jnp.concatenate([q_ref[qrows, h * HEAD_DIM:(h + 1) * HEAD_DIM] for h in heads], axis=0)
            s_all = lax.dot_general(qh, kband, (((1,), (1,)), ((), ())), preferred_element_type=F32)
            inv = []
            for c in range(0, n_rows, ATTN_ROWS):
                h = heads[c // BLOCK]
                s = s_all[c:c + ATTN_ROWS] * scale + tab_ref[h, c % BLOCK:c % BLOCK + ATTN_ROWS, :]
                s = jnp.where(outside, NEG_INF, s)
                sink = sink_ref[h]
                m = jnp.maximum(jnp.max(s, axis=-1, keepdims=True), sink)
                p = jnp.exp(s - m)
                denom = jnp.sum(p, axis=-1, keepdims=True) + jnp.exp(sink - m)
                p_ref[c:c + ATTN_ROWS, :] = p.astype(p_ref.dtype)
                inv.append(denom)
            o = jnp.dot(p_ref[...], vband, preferred_element_type=F32) / jnp.concatenate(inv, axis=0)
            for g, h in enumerate(heads):
                o_ref[qrows, h * HEAD_DIM:(h + 1) * HEAD_DIM] = o[g * BLOCK:(g + 1) * BLOCK].astype(o_ref.dtype)


def _attention(z, tab, sink, seq_len, attn_w, kv_w):
    rows = z.shape[0]
    nb = seq_len // BLOCK
    nq = math.gcd(nb, ATTN_BLOCKS_PER_STEP)
    nblk = rows // BLOCK
    n_kv = kv_w // HEAD_DIM
    q_per_kv = attn_w // kv_w
    assert attn_w % kv_w == 0
    kcol = attn_w // kv_w
    vcol = kcol + 1

    def prev(c):
        return lambda r: (jnp.maximum(r * nq - 1, 0), c)

    def main(c):
        return lambda r: (r, c)

    def nxt(c):
        return lambda r: (jnp.minimum((r + 1) * nq, nblk - 1), c)

    side_spec = lambda f, c: pl.BlockSpec((BLOCK, kv_w), f(c))
    main_spec = lambda c: pl.BlockSpec((nq * BLOCK, kv_w), main(c))
    return pl.pallas_call(
        functools.partial(_attn_kernel, nb=nb, nq=nq, n_kv=n_kv, q_per_kv=q_per_kv),
        grid=(nblk // nq,),
        in_specs=[pl.BlockSpec(memory_space=pltpu.SMEM),
                  pl.BlockSpec((nq * BLOCK, attn_w), lambda r: (r, 0)),
                  side_spec(prev, kcol), main_spec(kcol), side_spec(nxt, kcol),
                  side_spec(prev, vcol), main_spec(vcol), side_spec(nxt, vcol),
                  pl.BlockSpec(tab.shape, lambda r: (0, 0, 0))],
        out_specs=pl.BlockSpec((nq * BLOCK, attn_w), lambda r: (r, 0)),
        out_shape=jax.ShapeDtypeStruct((rows, attn_w), BF16),
        scratch_shapes=[pltpu.VMEM((q_per_kv * BLOCK, 3 * BLOCK), BF16)],
        compiler_params=_params(("parallel",)),
        name="windowed_gqa",
    )(sink.astype(F32), z, z, z, z, z, z, z, tab)


def _conv3_full(x, w, b, rows, n_rows):
    xm = jnp.where(rows == 0, 0.0, pltpu.roll(x, 1, 0))
    xp = jnp.where(rows == n_rows - 1, 0.0, pltpu.roll(x, n_rows - 1, 0))
    return xm * w[0:1] + x * w[1:2] + xp * w[2:3] + b


def _hy_pre_kernel(v_ref, x1_ref, x0_ref, wv_ref, w1_ref, w0_ref, bv_ref, b1_ref, b0_ref, u_ref, x0c_ref):
    n_rows = v_ref.shape[1]
    rows = lax.broadcasted_iota(jnp.int32, (n_rows, 1), 0)
    hv = _conv3_full(v_ref[0].astype(F32), wv_ref[...], bv_ref[...], rows, n_rows)
    hx1 = _conv3_full(x1_ref[0].astype(F32), w1_ref[...], b1_ref[...], rows, n_rows)
    u_ref[0] = (hv * hx1).astype(u_ref.dtype)
    hx0 = _conv3_full(x0_ref[0].astype(F32), w0_ref[...], b0_ref[...], rows, n_rows)
    x0c_ref[0] = hx0.astype(x0c_ref.dtype)


def _hy_pre(z3, short_w, short_b, hy_off, hw):
    bsz, seq_len, _ = z3.shape
    tc = LANES
    assert hy_off % tc == 0 and hw % tc == 0
    ob, nc = hy_off // tc, hw // tc
    zspec = lambda part: pl.BlockSpec((1, seq_len, tc), lambda b, c: (b, 0, ob + part * nc + c))
    wspec = lambda part: pl.BlockSpec((3, tc), lambda b, c: (0, part * nc + c))
    bspec = lambda part: pl.BlockSpec((1, tc), lambda b, c: (0, part * nc + c))
    ospec = pl.BlockSpec((1, seq_len, tc), lambda b, c: (b, 0, c))
    sw = short_w.astype(F32)
    sb = short_b.reshape(1, -1).astype(F32)
    return pl.pallas_call(
        _hy_pre_kernel,
        grid=(bsz, nc),
        in_specs=[zspec(0), zspec(1), zspec(2), wspec(0), wspec(1), wspec(2), bspec(0), bspec(1), bspec(2)],
        out_specs=[ospec, ospec],
        out_shape=[jax.ShapeDtypeStruct((bsz, seq_len, hw), BF16)] * 2,
        compiler_params=_params(("parallel", "parallel")),
        name="hyena_short_conv",
    )(z3, z3, z3, sw, sw, sw, sb, sb, sb)


def _filter_mlp_kernel(zf_ref, w1_ref, b1_ref, f1_ref, w2_ref, b2_ref, f2_ref, w3f_ref, w3b_ref,
                       decf_ref, decb_ref, hs_ref, hd_ref):
    tl = zf_ref.shape[0]
    hp = lax.Precision.HIGHEST
    zf = zf_ref[...]
    h = jnp.sin(f1_ref[...] * (jnp.dot(zf, w1_ref[...], precision=hp, preferred_element_type=F32) + b1_ref[...]))
    h = jnp.sin(f2_ref[...] * (jnp.dot(h, w2_ref[...], precision=hp, preferred_element_type=F32) + b2_ref[...]))
    t = zf[:, 0:1]
    fwd = jnp.dot(h, w3f_ref[...], precision=hp, preferred_element_type=F32) * jnp.exp(-t * decf_ref[...])
    bwd = jnp.dot(h, w3b_ref[...], precision=hp, preferred_element_type=F32) * jnp.exp(-t * decb_ref[...])
    pos = pl.program_id(0) * tl + lax.broadcasted_iota(jnp.int32, (tl, 1), 0)
    bwd = jnp.where(pos == 0, 0.0, bwd)
    hs_ref[...] = (fwd + bwd).astype(hs_ref.dtype)
    hd_ref[...] = (fwd - bwd).astype(hd_ref.dtype)


def _pad2(a, rows, cols):
    return jnp.pad(a.astype(F32), ((0, rows - a.shape[0]), (0, cols - a.shape[1])))


def _filter_mlp(seq_len, w1, b1, f1, w2, b2, f2, w3, decay):
    hw = decay.shape[1]
    pos = jnp.arange(seq_len, dtype=F32)
    t = pos / (seq_len - 1)
    bands = jnp.linspace(1e-4, FILTER_BANDS - 1, FILTER_BANDS, dtype=F32)
    ang = (2.0 * math.pi / seq_len) * pos[:, None] * bands[None, :]
    zf = jnp.concatenate([t[:, None], jnp.cos(ang), -jnp.sin(ang)], axis=-1)
    emb, hid = w1.shape
    assert emb <= LANES and hid <= LANES
    zf = _pad2(zf, seq_len, LANES)
    row = lambda v: _pad2(v.reshape(1, -1), 1, LANES)
    w3p = _pad2(w3, LANES, 2 * hw)
    dec = decay.reshape(1, 2 * hw).astype(F32)
    tl = _pick(seq_len, 512, 8)
    tc = _pick(hw, 2048, LANES)
    nc = hw // tc
    small = lambda shape: pl.BlockSpec(shape, lambda i, c: (0, 0))
    return pl.pallas_call(
        _filter_mlp_kernel,
        grid=(seq_len // tl, nc),
        in_specs=[pl.BlockSpec((tl, LANES), lambda i, c: (i, 0)),
                  small((LANES, LANES)), small((1, LANES)), small((1, LANES)),
                  small((LANES, LANES)), small((1, LANES)), small((1, LANES)),
                  pl.BlockSpec((LANES, tc), lambda i, c: (0, c)),
                  pl.BlockSpec((LANES, tc), lambda i, c: (0, nc + c)),
                  pl.BlockSpec((1, tc), lambda i, c: (0, c)),
                  pl.BlockSpec((1, tc), lambda i, c: (0, nc + c))],
        out_specs=[pl.BlockSpec((tl, tc), lambda i, c: (i, c))] * 2,
        out_shape=[jax.ShapeDtypeStruct((seq_len, hw), BF16)] * 2,
        compiler_params=_params(("parallel", "parallel")),
        name="hyena_filter_mlp",
    )(zf, _pad2(w1, LANES, LANES), row(b1), row(f1), _pad2(w2, LANES, LANES), row(b2), row(f2),
      w3p, w3p, dec, dec)


def _dft_mats_kernel(cb_ref, sb_ref, ck_ref, sk_ref, fc_ref, fs_ref, fst_ref):
    tk, n_cols = cb_ref.shape
    cb, sb = cb_ref[...], sb_ref[...]
    ck, sk = ck_ref[0], sk_ref[0]
    cs = ck * cb - sk * sb
    sn = sk * cb + ck * sb
    k = pl.program_id(0) * tk + lax.broadcasted_iota(jnp.int32, (tk, 1), 0)
    n = lax.broadcasted_iota(jnp.int32, (1, n_cols), 1)
    alt_n = jnp.where(n % 2 == 1, -1.0, 1.0)
    alt_k = jnp.where(k % 2 == 1, -1.0, 1.0)
    fc_ref[...] = cs.astype(fc_ref.dtype)
    fs_ref[...] = jnp.where(k == 0, alt_n, sn).astype(fs_ref.dtype)
    fst_ref[...] = jnp.where(n == 0, alt_k, sn).astype(fst_ref.dtype)


def _dft_mats(seq_len):
    n2 = 2 * seq_len
    tk = _pick(seq_len, 256, HALO)
    n_i = seq_len // tk
    n = jnp.arange(seq_len, dtype=jnp.int32)[None, :]

    def tables(k):
        ang = ((k * n) % n2).astype(F32) * (2.0 * math.pi / n2)
        return jnp.cos(ang), jnp.sin(ang)

    cb, sb = tables(jnp.arange(tk, dtype=jnp.int32)[:, None])
    ck, sk = tables(jnp.arange(n_i, dtype=jnp.int32)[:, None] * tk)
    base = pl.BlockSpec((tk, seq_len), lambda i: (0, 0))
    rowt = pl.BlockSpec((1, 1, seq_len), lambda i: (i, 0, 0))
    ospec = pl.BlockSpec((tk, seq_len), lambda i: (i, 0))
    return pl.pallas_call(
        _dft_mats_kernel,
        grid=(n_i,),
        in_specs=[base, base, rowt, rowt],
        out_specs=[ospec] * 3,
        out_shape=[jax.ShapeDtypeStruct((seq_len, seq_len), BF16)] * 3,
        compiler_params=_params(("parallel",)),
        name="dft_matrices",
    )(cb, sb, ck.reshape(n_i, 1, seq_len), sk.reshape(n_i, 1, seq_len))


def _filter_spectrum_kernel(fc_ref, fs_ref, hs_ref, hd_ref, a_ref, d_ref, *, seq_len):
    tm = fc_ref.shape[0]
    hs = hs_ref[...]
    row0 = (pl.program_id(0) * tm + lax.broadcasted_iota(jnp.int32, (tm, 1), 0)) == 0
    scale = jnp.where(row0, 1.0 / (2 * seq_len), 2.0 / (2 * seq_len))
    a_ref[...] = scale * jnp.dot(fc_ref[...], hs, preferred_element_type=F32)
    d_ref[...] = scale * jnp.dot(fs_ref[...], hd_ref[...], preferred_element_type=F32)

    @pl.when(pl.program_id(0) == 0)
    def _():
        nyq = jnp.dot(fs_ref[0:HALO, :], hs, preferred_element_type=F32)[0:1]
        d_ref[0:1, :] = nyq * (1.0 / (2 * seq_len))


def _filter_spectrum(fc, fs, hs, hd):
    seq_len, hw = hs.shape
    tm = _pick(seq_len, 512, HALO)
    tn = _pick(hw, 512, LANES)
    fspec = pl.BlockSpec((tm, seq_len), lambda i, j: (i, 0))
    hspec = pl.BlockSpec((seq_len, tn), lambda i, j: (0, j))
    ospec = pl.BlockSpec((tm, tn), lambda i, j: (i, j))
    return pl.pallas_call(
        functools.partial(_filter_spectrum_kernel, seq_len=seq_len),
        grid=(seq_len // tm, hw // tn),
        in_specs=[fspec, fspec, hspec, hspec],
        out_specs=[ospec, ospec],
        out_shape=[jax.ShapeDtypeStruct((seq_len, hw), F32)] * 2,
        compiler_params=_params(("parallel", "parallel")),
        name="hyena_filter_spectrum",
    )(fc, fs, hs, hd)


def _dft_fwd_kernel(fc_ref, fs_ref, u_ref, a_ref, d_ref, w1_ref, w2_ref):
    tm = fc_ref.shape[0]
    u = u_ref[0]
    p_all = jnp.dot(fc_ref[...], u, preferred_element_type=F32)
    q_all = jnp.dot(fs_ref[...], u, preferred_element_type=F32)
    sub = _pick(tm, EPILOGUE_ROWS, 8)
    for r0 in range(0, tm, sub):
        rs = slice(r0, r0 + sub)
        p, q, a, d = p_all[rs], q_all[rs], a_ref[rs, :], d_ref[rs, :]
        row0 = (pl.program_id(0) * tm + r0 + lax.broadcasted_iota(jnp.int32, (sub, 1), 0)) == 0
        qd = q * d
        w1_ref[0, rs, :] = (p * a - jnp.where(row0, 0.0, qd)).astype(w1_ref.dtype)
        w2_ref[0, rs, :] = jnp.where(row0, qd, p * d + q * a).astype(w2_ref.dtype)


def _dft_fwd(fc, fs, u, a, d):
    bsz, seq_len, hw = u.shape
    tm = _pick(seq_len, 512, HALO)
    tn = _pick(hw, 512, LANES)
    fspec = pl.BlockSpec((tm, seq_len), lambda i, j, b: (i, 0))
    kspec = pl.BlockSpec((tm, tn), lambda i, j, b: (i, j))
    ospec = pl.BlockSpec((1, tm, tn), lambda i, j, b: (b, i, j))
    return pl.pallas_call(
        _dft_fwd_kernel,
        grid=(seq_len // tm, hw // tn, bsz),
        in_specs=[fspec, fspec, pl.BlockSpec((1, seq_len, tn), lambda i, j, b: (b, 0, j)), kspec, kspec],
        out_specs=[ospec, ospec],
        out_shape=[jax.ShapeDtypeStruct((bsz, seq_len, hw), BF16)] * 2,
        compiler_params=_params(("parallel", "parallel", "parallel")),
        name="hyena_dft_forward",
    )(fc, fs, u, a, d)


def _dft_inv_kernel(fc_ref, fst_ref, w1_ref, w2_ref, u_ref, x0c_ref, skip_ref, o_ref):
    y1 = jnp.dot(fc_ref[...], w1_ref[0], preferred_element_type=F32)
    y2 = jnp.dot(fst_ref[...], w2_ref[0], preferred_element_type=F32)
    tm = o_ref.shape[1]
    sub = _pick(tm, EPILOGUE_ROWS, 8)
    skip = skip_ref[...]
    for r0 in range(0, tm, sub):
        rs = slice(r0, r0 + sub)
        y = y1[rs] + y2[rs] + u_ref[0, rs, :].astype(F32) * skip
        o_ref[0, rs, :] = (y * x0c_ref[0, rs, :].astype(F32)).astype(o_ref.dtype)


def _dft_inv(fc, fst, w1, w2, u, x0c, skip):
    bsz, seq_len, hw = u.shape
    tm = _pick(seq_len, 512, HALO)
    tn = _pick(hw, 512, LANES)
    fspec = pl.BlockSpec((tm, seq_len), lambda i, j, b: (i, 0))
    wspec = pl.BlockSpec((1, seq_len, tn), lambda i, j, b: (b, 0, j))
    tspec = pl.BlockSpec((1, tm, tn), lambda i, j, b: (b, i, j))
    return pl.pallas_call(
        _dft_inv_kernel,
        grid=(seq_len // tm, hw // tn, bsz),
        in_specs=[fspec, fspec, wspec, wspec, tspec, tspec, pl.BlockSpec((1, tn), lambda i, j, b: (0, j))],
        out_specs=tspec,
        out_shape=jax.ShapeDtypeStruct((bsz, seq_len, hw), BF16),
        compiler_params=_params(("parallel", "parallel", "parallel")),
        name="hyena_dft_inverse",
    )(fc, fst, w1, w2, u, x0c, skip.reshape(1, hw).astype(F32))


def _ffn_up_kernel(x_ref, xp_ref, xn_ref, g_ref, wg_ref, wv_ref, cw_ref, cb_ref, o_ref, h_ref, halo_ref,
                   *, seq_len, n_j):
    s = pl.program_id(0)
    tm = x_ref.shape[0]

    @pl.when(s % n_j == 0)
    def _():
        g = g_ref[...]
        _rms_rows_to(x_ref, g_ref, h_ref)
        halo_ref[0:HALO] = _rms_scale(xp_ref[...], g).astype(halo_ref.dtype)
        halo_ref[HALO:2 * HALO] = _rms_scale(xn_ref[...], g).astype(halo_ref.dtype)

    h = h_ref[...]
    wg = wg_ref[...]
    gu = jnp.dot(h, wg, preferred_element_type=F32)
    val = jnp.dot(h, wv_ref[...], preferred_element_type=F32)
    hg = jnp.dot(halo_ref[...], wg, preferred_element_type=F32)
    row_start = (s // n_j) * tm
    above = jnp.where(row_start % seq_len == 0, 0.0, hg[HALO - 8:HALO])
    below = jnp.where((row_start + tm) % seq_len == 0, 0.0, hg[HALO:HALO + 8])
    cw = cw_ref[...]
    cb = cb_ref[...]
    sub = _pick(tm, EPILOGUE_ROWS, 8)
    ext = sub + 16
    for r0 in range(0, tm, sub):
        before = above if r0 == 0 else gu[r0 - 8:r0]
        after = below if r0 + sub == tm else gu[r0 + sub:r0 + sub + 8]
        g_ext = jnp.concatenate([before, gu[r0:r0 + sub], after], axis=0)
        g_m1 = pltpu.roll(g_ext, 1, 0)[8:8 + sub]
        g_p1 = pltpu.roll(g_ext, ext - 1, 0)[8:8 + sub]
        conv = g_m1 * cw[0:1] + g_ext[8:8 + sub] * cw[1:2] + g_p1 * cw[2:3] + cb
        gelu = 0.5 * conv * (1.0 + lax.erf(conv * math.sqrt(0.5)))
        o_ref[r0:r0 + sub, :] = (gelu * val[r0:r0 + sub]).astype(o_ref.dtype)


def _ffn_up(x, gain, w_up, conv_w, conv_b, seq_len):
    rows, d = x.shape
    d_ff = w_up.shape[1] // 2
    tm = _pick(seq_len, 1024, HALO)
    tn = _pick(d_ff, 256, LANES)
    n_i, n_j = rows // tm, d_ff // tn
    hb = tm // HALO
    n_halo = rows // HALO
    return pl.pallas_call(
        functools.partial(_ffn_up_kernel, seq_len=seq_len, n_j=n_j),
        grid=(n_i * n_j,),
        in_specs=[pl.BlockSpec((tm, d), lambda s: (s // n_j, 0), pipeline_mode=pl.Buffered(1)),
                  pl.BlockSpec((HALO, d), lambda s: (jnp.maximum((s // n_j) * hb - 1, 0), 0)),
                  pl.BlockSpec((HALO, d), lambda s: (jnp.minimum((s // n_j + 1) * hb, n_halo - 1), 0)),
                  pl.BlockSpec((1, d), lambda s: (0, 0)),
                  pl.BlockSpec((d, tn), lambda s: (0, s % n_j)),
                  pl.BlockSpec((d, tn), lambda s: (0, n_j + s % n_j)),
                  pl.BlockSpec((3, tn), lambda s: (0, s % n_j)),
                  pl.BlockSpec((1, tn), lambda s: (0, s % n_j))],
        out_specs=pl.BlockSpec((tm, tn), lambda s: (s // n_j, s % n_j)),
        out_shape=jax.ShapeDtypeStruct((rows, d_ff), BF16),
        scratch_shapes=[pltpu.VMEM((tm, d), BF16), pltpu.VMEM((2 * HALO, d), BF16)],
        compiler_params=_params(("arbitrary",)),
        name="convglu_up",
    )(x, x, x, gain.reshape(1, d).astype(F32), w_up, w_up, conv_w.astype(F32),
      conv_b.reshape(1, d_ff).astype(F32))


def _prep_layer(w):
    names = ("w_in", "w_attn_o", "w_hyena_o", "w_out", "w_up", "w_down", "w_ple_gate", "w_ple")
    return {k: w[k].astype(BF16) for k in names}


def _encoder_layer(x, p, tab, w, wp, seq_len):
    rows, d_model = x.shape
    bsz = rows // seq_len
    attn_w = w["w_attn_o"].shape[0]
    hw = w["w_hyena_o"].shape[0]
    in_cols = w["w_in"].shape[1]
    kv_w = (in_cols - attn_w - 3 * hw - 2 * d_model) // 2
    hy_off = attn_w + 2 * kv_w
    gate_off = hy_off + 3 * hw
    tm = _pick(seq_len, 1024, HALO)

    z = _fused_matmul([(x, wp["w_in"])], [], _ep_plain, BF16, in_cols, tm, _pick(in_cols, 1024, LANES),
                      "in_proj", norm_gain=w["g_mix"])
    attn = _attention(z, tab, w["attn_sink"], seq_len, attn_w, kv_w)

    u, x0c = _hy_pre(z.reshape(bsz, seq_len, in_cols), w["hy_short_w"], w["hy_short_b"], hy_off, hw)
    hs, hd = _filter_mlp(seq_len, w["hy_filt_w1"], w["hy_filt_b1"], w["hy_filt_f1"], w["hy_filt_w2"],
                         w["hy_filt_b2"], w["hy_filt_f2"], w["hy_filt_w3"], w["hy_decay"])
    fc, fs, fst = _dft_mats(seq_len)
    ka, kd = _filter_spectrum(fc, fs, hs, hd)
    w1, w2 = _dft_fwd(fc, fs, u, ka, kd)
    hy_out = _dft_inv(fc, fst, w1, w2, u, x0c, w["hy_skip"]).reshape(rows, hw)

    tn = _pick(math.gcd(d_model, gate_off), 512, LANES)
    merged = _fused_matmul([(attn, wp["w_attn_o"]), (hy_out, wp["w_hyena_o"])],
                           [(z, "tile", gate_off), (z, "tile", gate_off + d_model)],
                           _ep_branch_merge, BF16, d_model, tm, tn, "branch_merge")
    tn = _pick(d_model, 512, LANES)
    x = _fused_matmul([(merged, wp["w_out"])], [(x, "tile", 0)], _ep_residual, F32, d_model, tm, tn, "out_proj")

    act = _ffn_up(x, w["g_ffn"], wp["w_up"], w["ffn_conv_w"], w["ffn_conv_b"], seq_len)
    x = _fused_matmul([(act, wp["w_down"])], [(x, "tile", 0)], _ep_residual, F32, d_model,
                      _pick(seq_len, 512, HALO), _pick(d_model, 512, LANES), "ffn_down")

    x = _fused_matmul([(x, wp["w_ple_gate"]), (p.astype(BF16), wp["w_ple"])], [(x, "tile", 0)],
                      _ep_ple, F32, d_model, tm, tn, "ple_gate", norm_gain=w["g_ple"])
    return x


_LAYER_WEIGHT_NAMES = ("g_mix", "w_in", "attn_sink", "hy_short_w", "hy_short_b", "hy_filt_w1", "hy_filt_b1",
                       "hy_filt_f1", "hy_filt_w2", "hy_filt_b2", "hy_filt_f2", "hy_filt_w3", "hy_decay",
                       "hy_skip", "w_attn_o", "w_hyena_o", "w_out", "g_ffn", "w_up", "ffn_conv_w", "ffn_conv_b",
                       "w_down", "g_ple", "w_ple_gate", "w_ple")


def _trunk(x, p, tab, layers, preps, g_final):
    bsz, seq_len, d_model = x.shape
    xf = x.reshape(bsz * seq_len, d_model)
    for l, (w, wp) in enumerate(zip(layers, preps)):
        xf = _encoder_layer(xf, p[l].reshape(bsz * seq_len, -1), tab, w, wp, seq_len)
    return _rmsnorm(xf, g_final, x.dtype).reshape(bsz, seq_len, d_model)


def kernel(x_prompt, x_sample, p_prompt, p_sample, rel_bias, g_mix, w_in, attn_sink, hy_short_w, hy_short_b, hy_filt_w1, hy_filt_b1, hy_filt_f1, hy_filt_w2, hy_filt_b2, hy_filt_f2, hy_filt_w3, hy_decay, hy_skip, w_attn_o, w_hyena_o, w_out, g_ffn, w_up, ffn_conv_w, ffn_conv_b, w_down, g_ple, w_ple_gate, w_ple, g_final):
    stacked = (g_mix, w_in, attn_sink, hy_short_w, hy_short_b, hy_filt_w1, hy_filt_b1, hy_filt_f1, hy_filt_w2,
               hy_filt_b2, hy_filt_f2, hy_filt_w3, hy_decay, hy_skip, w_attn_o, w_hyena_o, w_out, g_ffn, w_up,
               ffn_conv_w, ffn_conv_b, w_down, g_ple, w_ple_gate, w_ple)
    depth = g_mix.shape[0]
    layers = [dict(zip(_LAYER_WEIGHT_NAMES, [a[l] for a in stacked])) for l in range(depth)]
    preps = [_prep_layer(w) for w in layers]
    tab = _bias_table(rel_bias)
    y_prompt = _trunk(x_prompt, p_prompt, tab, layers, preps, g_final)
    y_sample = _trunk(x_sample, p_sample, tab, layers, preps, g_final)
    return (y_prompt, y_sample)
```

```python
import functools
import math

import jax
import jax.numpy as jnp
from jax import lax
from jax.experimental import pallas as pl
from jax.experimental.pallas import tpu as pltpu

F32 = jnp.float32
BF16 = jnp.bfloat16

EPS = 1e-6
NEG_INF = -1e30
HEAD_DIM = 128
WINDOW = 128
BLOCK = 128
N_BUCKETS = 32
MAX_DISTANCE = 128
FILTER_BANDS = 16

V7X_VMEM_LIMIT_BYTES = 56 * 1024 * 1024
LANES = 128
HALO = 16
EPILOGUE_ROWS = 32
NORM_ROWS = 32
ATTN_ROWS = 32
ATTN_BLOCKS_PER_STEP = 4
PERM_ROWS = 256


def _pick(n, pref, align):
    t = min(pref, n)
    t -= t % align
    while t >= align:
        if n % t == 0:
            return t
        t -= align
    return n


def _params(sem):
    return pltpu.CompilerParams(dimension_semantics=sem, vmem_limit_bytes=V7X_VMEM_LIMIT_BYTES)


def _rms_scale(x, g):
    ms = jnp.mean(x * x, axis=-1, keepdims=True)
    return x * lax.rsqrt(ms + EPS) * g


def _rms_rows_to(x_ref, g_ref, h_ref):
    rows = x_ref.shape[0]
    chunk = _pick(rows, NORM_ROWS, 8)
    g = g_ref[...]

    def body(c, carry):
        r = pl.ds(pl.multiple_of(c * chunk, chunk), chunk)
        h_ref[r, :] = _rms_scale(x_ref[r, :].astype(F32), g).astype(h_ref.dtype)
        return carry

    n_chunks = rows // chunk
    lax.fori_loop(0, n_chunks, body, 0, unroll=math.gcd(n_chunks, 8))


def _rmsnorm_kernel(x_ref, g_ref, o_ref):
    o_ref[...] = _rms_scale(x_ref[...].astype(F32), g_ref[...]).astype(o_ref.dtype)


def _rmsnorm(x, g, out_dtype):
    rows, d = x.shape
    tm = _pick(rows, 512, 8)
    return pl.pallas_call(
        _rmsnorm_kernel,
        grid=(rows // tm,),
        in_specs=[pl.BlockSpec((tm, d), lambda i: (i, 0)), pl.BlockSpec((1, d), lambda i: (0, 0))],
        out_specs=pl.BlockSpec((tm, d), lambda i: (i, 0)),
        out_shape=jax.ShapeDtypeStruct((rows, d), out_dtype),
        compiler_params=_params(("parallel",)),
        name="rmsnorm",
    )(x, g.reshape(1, d).astype(F32))


def _fused_matmul_kernel(*refs, n_pairs, n_extras, epilogue, norm, n_j):
    s = pl.program_id(0)
    pos = 0
    a_refs, b_refs = [], []
    for p in range(n_pairs):
        a_refs.append(refs[pos])
        pos += 1
        if norm and p == 0:
            g_ref = refs[pos]
            pos += 1
        b_refs.append(refs[pos])
        pos += 1
    extra_refs = refs[pos:pos + n_extras]
    o_ref = refs[pos + n_extras]
    if norm:
        h_ref = refs[pos + n_extras + 1]

        @pl.when(s % n_j == 0)
        def _():
            _rms_rows_to(a_refs[0], g_ref, h_ref)

        a_refs[0] = h_ref

    dots = [jnp.dot(a_refs[p][...], b_refs[p][...], preferred_element_type=F32) for p in range(n_pairs)]
    tm = o_ref.shape[0]
    sub = _pick(tm, EPILOGUE_ROWS, 8)
    for r0 in range(0, tm, sub):
        rs = slice(r0, r0 + sub)
        extras = [r[rs, :] if r.shape[0] == tm else r[...] for r in extra_refs]
        o_ref[rs, :] = epilogue([d[rs] for d in dots], extras).astype(o_ref.dtype)


def _fused_matmul(pairs, extras, epilogue, out_dtype, n_cols, tm, tn, name, norm_gain=None):
    rows = pairs[0][0].shape[0]
    assert rows % tm == 0 and n_cols % tn == 0
    n_i, n_j = rows // tm, n_cols // tn
    norm = norm_gain is not None
    args, in_specs, scratch = [], [], []
    for p, (a, b) in enumerate(pairs):
        k = a.shape[1]
        assert b.shape[0] == k
        if norm and p == 0:
            args += [a, norm_gain.reshape(1, k).astype(F32), b]
            in_specs += [pl.BlockSpec((tm, k), lambda s: (s // n_j, 0), pipeline_mode=pl.Buffered(1)),
                         pl.BlockSpec((1, k), lambda s: (0, 0))]
            scratch.append(pltpu.VMEM((tm, k), BF16))
        else:
            args += [a, b]
            in_specs.append(pl.BlockSpec((tm, k), lambda s: (s // n_j, 0)))
        in_specs.append(pl.BlockSpec((k, tn), lambda s: (0, s % n_j)))
    for arr, kind, off in extras:
        assert off % tn == 0
        ob = off // tn
        args.append(arr)
        if kind == "tile":
            in_specs.append(pl.BlockSpec((tm, tn), lambda s, ob=ob: (s // n_j, s % n_j + ob)))
        else:
            in_specs.append(pl.BlockSpec((1, tn), lambda s, ob=ob: (0, s % n_j + ob)))
    return pl.pallas_call(
        functools.partial(_fused_matmul_kernel, n_pairs=len(pairs), n_extras=len(extras), epilogue=epilogue,
                          norm=norm, n_j=n_j),
        grid=(n_i * n_j,),
        in_specs=in_specs,
        out_specs=pl.BlockSpec((tm, tn), lambda s: (s // n_j, s % n_j)),
        out_shape=jax.ShapeDtypeStruct((rows, n_cols), out_dtype),
        scratch_shapes=scratch,
        compiler_params=_params(("arbitrary",)),
        name=name,
    )(*args)


def _ep_plain(dots, extras):
    return dots[0]


def _sigmoid(x):
    return 0.5 * (1.0 + jnp.tanh(0.5 * x))


def _ep_branch_merge(dots, extras):
    ga, gh = extras
    return _sigmoid(ga.astype(F32)) * dots[0] + _sigmoid(gh.astype(F32)) * dots[1]


def _ep_residual(dots, extras):
    return extras[0] + dots[0]


def _ep_ple(dots, extras):
    return extras[0] + _sigmoid(dots[0]) * dots[1]


def _bias_table_kernel(rb_ref, bkt_ref, o_ref):
    h = pl.program_id(0)
    bkt = bkt_ref[...]
    q = lax.broadcasted_iota(jnp.int32, bkt.shape, 0)
    s = lax.broadcasted_iota(jnp.int32, bkt.shape, 1)
    rel = s - BLOCK - q
    acc = jnp.zeros(bkt.shape, F32)
    for b in range(N_BUCKETS):
        acc = jnp.where(bkt == b, rb_ref[b, h], acc)
    o_ref[0] = jnp.where(jnp.abs(rel) <= WINDOW, acc, NEG_INF)


def _bias_table(rel_bias):
    n_heads = rel_bias.shape[1]
    half = N_BUCKETS // 2
    max_exact = half // 2
    qi = jnp.arange(BLOCK)[:, None]
    sj = jnp.arange(3 * BLOCK)[None, :]
    rel = sj - BLOCK - qi
    ret = jnp.where(rel > 0, half, 0)
    n = jnp.abs(rel)
    nf = jnp.maximum(n, 1).astype(F32)
    large = max_exact + (jnp.log(nf / max_exact) / math.log(MAX_DISTANCE / max_exact)
                         * (half - max_exact)).astype(jnp.int32)
    large = jnp.minimum(large, half - 1)
    bucket = (ret + jnp.where(n < max_exact, n, large)).astype(jnp.int32)
    return pl.pallas_call(
        _bias_table_kernel,
        grid=(n_heads,),
        in_specs=[pl.BlockSpec(memory_space=pltpu.SMEM),
                  pl.BlockSpec((BLOCK, 3 * BLOCK), lambda h: (0, 0))],
        out_specs=pl.BlockSpec((1, BLOCK, 3 * BLOCK), lambda h: (h, 0, 0)),
        out_shape=jax.ShapeDtypeStruct((n_heads, BLOCK, 3 * BLOCK), F32),
        compiler_params=_params(("arbitrary",)),
        name="attn_bias_table",
    )(rel_bias.astype(F32), bucket)


def _attn_kernel(sink_ref, q_ref, kp_ref, km_ref, kn_ref, vp_ref, vm_ref, vn_ref, tab_ref, o_ref, p_ref,
                 *, nb, nq, n_kv, q_per_kv):
    first = (pl.program_id(0) * nq) % nb
    col = lax.broadcasted_iota(jnp.int32, (1, 3 * BLOCK), 1)
    scale = HEAD_DIM ** -0.5
    n_rows = q_per_kv * BLOCK

    def band(prev_ref, main_ref, next_ref, j, ks):
        parts = []
        for b in (j - 1, j, j + 1):
            if b < 0:
                parts.append(prev_ref[:, ks])
            elif b >= nq:
                parts.append(next_ref[:, ks])
            else:
                parts.append(main_ref[b * BLOCK:(b + 1) * BLOCK, ks])
        return jnp.concatenate(parts, axis=0)

    for j in range(nq):
        n = first + j
        outside = ((col < BLOCK) & (n == 0)) | ((col >= 2 * BLOCK) & (n == nb - 1))
        qrows = slice(j * BLOCK, (j + 1) * BLOCK)
        for kh in range(n_kv):
            ks = slice(kh * HEAD_DIM, (kh + 1) * HEAD_DIM)
            kband = band(kp_ref, km_ref, kn_ref, j, ks)
            vband = band(vp_ref, vm_ref, vn_ref, j, ks)
            heads = list(range(kh * q_per_kv, (kh + 1) * q_per_kv))
            qh = jnp.concatenate([q_ref[qrows, h * HEAD_DIM:(h + 1) * HEAD_DIM] for h in heads], axis=0)
            s_all = lax.dot_general(qh, kband, (((1,), (1,)), ((), ())), preferred_element_type=F32)
            inv = []
            for c in range(0, n_rows, ATTN_ROWS):
                h = heads[c // BLOCK]
                s = s_all[c:c + ATTN_ROWS] * scale + tab_ref[h, c % BLOCK:c % BLOCK + ATTN_ROWS, :]
                s = jnp.where(outside, NEG_INF, s)
                sink = sink_ref[h]
                m = jnp.maximum(jnp.max(s, axis=-1, keepdims=True), sink)
                p = jnp.exp(s - m)
                denom = jnp.sum(p, axis=-1, keepdims=True) + jnp.exp(sink - m)
                p_ref[c:c + ATTN_ROWS, :] = p.astype(p_ref.dtype)
                inv.append(denom)
            o = jnp.dot(p_ref[...], vband, preferred_element_type=F32) / jnp.concatenate(inv, axis=0)
            for g, h in enumerate(heads):
                o_ref[qrows, h * HEAD_DIM:(h + 1) * HEAD_DIM] = o[g * BLOCK:(g + 1) * BLOCK].astype(o_ref.dtype)


def _attention(z, tab, sink, seq_len, attn_w, kv_w):
    rows = z.shape[0]
    nb = seq_len // BLOCK
    nq = math.gcd(nb, ATTN_BLOCKS_PER_STEP)
    nblk = rows // BLOCK
    n_kv = kv_w // HEAD_DIM
    q_per_kv = attn_w // kv_w
    assert attn_w % kv_w == 0
    kcol = attn_w // kv_w
    vcol = kcol + 1

    def prev(c):
        return lambda r: (jnp.maximum(r * nq - 1, 0), c)

    def main(c):
        return lambda r: (r, c)

    def nxt(c):
        return lambda r: (jnp.minimum((r + 1) * nq, nblk - 1), c)

    side_spec = lambda f, c: pl.BlockSpec((BLOCK, kv_w), f(c))
    main_spec = lambda c: pl.BlockSpec((nq * BLOCK, kv_w), main(c))
    return pl.pallas_call(
        functools.partial(_attn_kernel, nb=nb, nq=nq, n_kv=n_kv, q_per_kv=q_per_kv),
        grid=(nblk // nq,),
        in_specs=[pl.BlockSpec(memory_space=pltpu.SMEM),
                  pl.BlockSpec((nq * BLOCK, attn_w), lambda r: (r, 0)),
                  side_spec(prev, kcol), main_spec(kcol), side_spec(nxt, kcol),
                  side_spec(prev, vcol), main_spec(vcol), side_spec(nxt, vcol),
                  pl.BlockSpec(tab.shape, lambda r: (0, 0, 0))],
        out_specs=pl.BlockSpec((nq * BLOCK, attn_w), lambda r: (r, 0)),
        out_shape=jax.ShapeDtypeStruct((rows, attn_w), BF16),
        scratch_shapes=[pltpu.VMEM((q_per_kv * BLOCK, 3 * BLOCK), BF16)],
        compiler_params=_params(("parallel",)),
        name="windowed_gqa",
    )(sink.astype(F32), z, z, z, z, z, z, z, tab)


def _conv3_full(x, w, b, rows, n_rows):
    xm = jnp.where(rows == 0, 0.0, pltpu.roll(x, 1, 0))
    xp = jnp.where(rows == n_rows - 1, 0.0, pltpu.roll(x, n_rows - 1, 0))
    return xm * w[0:1] + x * w[1:2] + xp * w[2:3] + b


def _negate_index(x):
    n, _ = x.shape
    rb = min(n, PERM_ROWS)
    nbk = n // rb
    r = lax.broadcasted_iota(jnp.int32, (rb, rb), 0)
    c = lax.broadcasted_iota(jnp.int32, (rb, rb), 1)
    flip = (r + c == rb).astype(x.dtype)
    first = lax.broadcasted_iota(jnp.int32, (rb, 1), 0) == 0
    outs = []
    for bj in range(nbk):
        a = x[(nbk - 1 - bj) * rb:(nbk - bj) * rb]
        b0 = ((nbk - bj) % nbk) * rb
        out = jnp.dot(flip, a, preferred_element_type=F32).astype(x.dtype)
        outs.append(jnp.where(first, x[b0:b0 + 1], out))
    return jnp.concatenate(outs, axis=0)


def _hy_pre_kernel(v_ref, x1_ref, x0_ref, wv_ref, w1_ref, w0_ref, bv_ref, b1_ref, b0_ref,
                   ua_ref, ub_ref, xa_ref, xb_ref, e_ref, o_ref, side_ref):
    n_rows = v_ref.shape[1]
    half = n_rows // 2
    rows = lax.broadcasted_iota(jnp.int32, (n_rows, 1), 0)
    hv = _conv3_full(v_ref[0].astype(F32), wv_ref[...], bv_ref[...], rows, n_rows)
    hx1 = _conv3_full(x1_ref[0].astype(F32), w1_ref[...], b1_ref[...], rows, n_rows)
    u = (hv * hx1).astype(ua_ref.dtype)
    hx0 = _conv3_full(x0_ref[0].astype(F32), w0_ref[...], b0_ref[...], rows, n_rows).astype(xa_ref.dtype)
    ua_ref[0] = u[0:half]
    xa_ref[0] = hx0[0:half]
    ub = _negate_index(u[half:n_rows])
    ub_ref[0] = ub
    xb_ref[0] = _negate_index(hx0[half:n_rows])
    ua32 = u[0:half].astype(F32)
    ub32 = ub.astype(F32)
    first = rows[0:half] == 0
    e_ref[0] = jnp.where(first, ua32, ua32 + ub32).astype(e_ref.dtype)
    o_ref[0] = jnp.where(first, ua32, ua32 - ub32).astype(o_ref.dtype)
    u32 = u.astype(F32)
    nyq = jnp.sum(jnp.where(rows % 2 == 1, -u32, u32), axis=0, keepdims=True)
    side_ref[0] = jnp.concatenate([u32[half:half + 1], nyq, jnp.zeros((6, u32.shape[1]), F32)], axis=0)


def _hy_pre(z3, short_w, short_b, hy_off, hw):
    bsz, seq_len, _ = z3.shape
    half = seq_len // 2
    tc = LANES
    assert hy_off % tc == 0 and hw % tc == 0
    ob, nc = hy_off // tc, hw // tc
    zspec = lambda part: pl.BlockSpec((1, seq_len, tc), lambda b, c: (b, 0, ob + part * nc + c))
    wspec = lambda part: pl.BlockSpec((3, tc), lambda b, c: (0, part * nc + c))
    bspec = lambda part: pl.BlockSpec((1, tc), lambda b, c: (0, part * nc + c))
    ospec = pl.BlockSpec((1, half, tc), lambda b, c: (b, 0, c))
    sw = short_w.astype(F32)
    sb = short_b.reshape(1, -1).astype(F32)
    return pl.pallas_call(
        _hy_pre_kernel,
        grid=(bsz, nc),
        in_specs=[zspec(0), zspec(1), zspec(2), wspec(0), wspec(1), wspec(2), bspec(0), bspec(1), bspec(2)],
        out_specs=[ospec] * 6 + [pl.BlockSpec((1, 8, tc), lambda b, c: (b, 0, c))],
        out_shape=[jax.ShapeDtypeStruct((bsz, half, hw), BF16)] * 6 + [jax.ShapeDtypeStruct((bsz, 8, hw), F32)],
        compiler_params=_params(("parallel", "parallel")),
        name="hyena_short_conv",
    )(z3, z3, z3, sw, sw, sw, sb, sb, sb)


def _filter_mlp_kernel(zf_ref, w1_ref, b1_ref, f1_ref, w2_ref, b2_ref, f2_ref, w3f_ref, w3b_ref,
                       decf_ref, decb_ref, hs_ref, hd_ref):
    tl = zf_ref.shape[0]
    hp = lax.Precision.HIGHEST
    zf = zf_ref[...]
    h = jnp.sin(f1_ref[...] * (jnp.dot(zf, w1_ref[...], precision=hp, preferred_element_type=F32) + b1_ref[...]))
    h = jnp.sin(f2_ref[...] * (jnp.dot(h, w2_ref[...], precision=hp, preferred_element_type=F32) + b2_ref[...]))
    t = zf[:, 0:1]
    fwd = jnp.dot(h, w3f_ref[...], precision=hp, preferred_element_type=F32) * jnp.exp(-t * decf_ref[...])
    bwd = jnp.dot(h, w3b_ref[...], precision=hp, preferred_element_type=F32) * jnp.exp(-t * decb_ref[...])
    pos = pl.program_id(0) * tl + lax.broadcasted_iota(jnp.int32, (tl, 1), 0)
    bwd = jnp.where(pos == 0, 0.0, bwd)
    hs_ref[...] = (fwd + bwd).astype(hs_ref.dtype)
    hd_ref[...] = (fwd - bwd).astype(hd_ref.dtype)


def _pad2(a, rows, cols):
    return jnp.pad(a.astype(F32), ((0, rows - a.shape[0]), (0, cols - a.shape[1])))


def _filter_mlp(seq_len, w1, b1, f1, w2, b2, f2, w3, decay):
    hw = decay.shape[1]
    pos = jnp.arange(seq_len, dtype=F32)
    t = pos / (seq_len - 1)
    bands = jnp.linspace(1e-4, FILTER_BANDS - 1, FILTER_BANDS, dtype=F32)
    ang = (2.0 * math.pi / seq_len) * pos[:, None] * bands[None, :]
    zf = jnp.concatenate([t[:, None], jnp.cos(ang), -jnp.sin(ang)], axis=-1)
    emb, hid = w1.shape
    assert emb <= LANES and hid <= LANES
    zf = _pad2(zf, seq_len, LANES)
    row = lambda v: _pad2(v.reshape(1, -1), 1, LANES)
    w3p = _pad2(w3, LANES, 2 * hw)
    dec = decay.reshape(1, 2 * hw).astype(F32)
    tl = _pick(seq_len, 512, 8)
    tc = _pick(hw, 2048, LANES)
    nc = hw // tc
    small = lambda shape: pl.BlockSpec(shape, lambda i, c: (0, 0))
    return pl.pallas_call(
        _filter_mlp_kernel,
        grid=(seq_len // tl, nc),
        in_specs=[pl.BlockSpec((tl, LANES), lambda i, c: (i, 0)),
                  small((LANES, LANES)), small((1, LANES)), small((1, LANES)),
                  small((LANES, LANES)), small((1, LANES)), small((1, LANES)),
                  pl.BlockSpec((LANES, tc), lambda i, c: (0, c)),
                  pl.BlockSpec((LANES, tc), lambda i, c: (0, nc + c)),
                  pl.BlockSpec((1, tc), lambda i, c: (0, c)),
                  pl.BlockSpec((1, tc), lambda i, c: (0, nc + c))],
        out_specs=[pl.BlockSpec((tl, tc), lambda i, c: (i, c))] * 2,
        out_shape=[jax.ShapeDtypeStruct((seq_len, hw), BF16)] * 2,
        compiler_params=_params(("parallel", "parallel")),
        name="hyena_filter_mlp",
    )(zf, _pad2(w1, LANES, LANES), row(b1), row(f1), _pad2(w2, LANES, LANES), row(b2), row(f2),
      w3p, w3p, dec, dec)


def _trig_kernel(cb_ref, sb_ref, ck_ref, sk_ref, c_ref, s_ref, *, nyquist_row):
    tk, n_cols = cb_ref.shape
    cb, sb = cb_ref[...], sb_ref[...]
    ck, sk = ck_ref[0], sk_ref[0]
    sn = sk * cb + ck * sb
    if nyquist_row:
        k = pl.program_id(0) * tk + lax.broadcasted_iota(jnp.int32, (tk, 1), 0)
        n = lax.broadcasted_iota(jnp.int32, (1, n_cols), 1)
        sn = jnp.where(k == 0, jnp.where(n % 2 == 1, -1.0, 1.0), sn)
    c_ref[...] = (ck * cb - sk * sb).astype(c_ref.dtype)
    s_ref[...] = sn.astype(s_ref.dtype)


def _trig_matrices(n, n2, base_idx, tile_idx, name, nyquist_row=False):
    tk = _pick(n, 256, HALO)
    n_i = n // tk
    col = jnp.arange(n, dtype=jnp.int32)[None, :]

    def tables(idx):
        ang = (idx % n2).astype(F32) * (2.0 * math.pi / n2)
        return jnp.cos(ang), jnp.sin(ang)

    cb, sb = tables(base_idx(jnp.arange(tk, dtype=jnp.int32)[:, None], col))
    ck, sk = tables(tile_idx(jnp.arange(n_i, dtype=jnp.int32)[:, None] * tk, col))
    base = pl.BlockSpec((tk, n), lambda i: (0, 0))
    rowt = pl.BlockSpec((1, 1, n), lambda i: (i, 0, 0))
    ospec = pl.BlockSpec((tk, n), lambda i: (i, 0))
    return pl.pallas_call(
        functools.partial(_trig_kernel, nyquist_row=nyquist_row),
        grid=(n_i,),
        in_specs=[base, base, rowt, rowt],
        out_specs=[ospec] * 2,
        out_shape=[jax.ShapeDtypeStruct((n, n), BF16)] * 2,
        compiler_params=_params(("parallel",)),
        name=name,
    )(cb, sb, ck.reshape(n_i, 1, n), sk.reshape(n_i, 1, n))


def _dft_mats(seq_len):
    return _trig_matrices(seq_len, 2 * seq_len, lambda d, c: d * c, lambda r, c: r * c, "dft_matrices",
                          nyquist_row=True)


def _folded_dft_mats(seq_len):
    half, n2 = seq_len // 2, 2 * seq_len
    ce, se = _trig_matrices(half, n2, lambda d, c: 2 * d * c, lambda r, c: 2 * r * c, "dft_even_bins")
    co, so = _trig_matrices(half, n2, lambda d, c: (2 * d + 1) * c, lambda r, c: 2 * r * c, "dft_odd_bins")
    cot, sot = _trig_matrices(half, n2, lambda d, c: (2 * c + 1) * d, lambda r, c: (2 * c + 1) * r,
                              "dft_odd_bins_t")
    return ce, se, co, so, cot, sot


def _filter_spectrum_kernel(fc_ref, fs_ref, hs_ref, hd_ref, a_ref, d_ref, *, seq_len):
    tm = fc_ref.shape[0]
    hs = hs_ref[...]
    row0 = (pl.program_id(0) * tm + lax.broadcasted_iota(jnp.int32, (tm, 1), 0)) == 0
    scale = jnp.where(row0, 1.0 / (2 * seq_len), 2.0 / (2 * seq_len))
    a_ref[...] = scale * jnp.dot(fc_ref[...], hs, preferred_element_type=F32)
    d_ref[...] = scale * jnp.dot(fs_ref[...], hd_ref[...], preferred_element_type=F32)

    @pl.when(pl.program_id(0) == 0)
    def _():
        nyq = jnp.dot(fs_ref[0:HALO, :], hs, preferred_element_type=F32)[0:1]
        d_ref[0:1, :] = nyq * (1.0 / (2 * seq_len))


def _filter_spectrum(fc, fs, hs, hd):
    seq_len, hw = hs.shape
    tm = _pick(seq_len, 512, HALO)
    tn = _pick(hw, 512, LANES)
    fspec = pl.BlockSpec((tm, seq_len), lambda i, j: (i, 0))
    hspec = pl.BlockSpec((seq_len, tn), lambda i, j: (0, j))
    ospec = pl.BlockSpec((tm, tn), lambda i, j: (i, j))
    return pl.pallas_call(
        functools.partial(_filter_spectrum_kernel, seq_len=seq_len),
        grid=(seq_len // tm, hw // tn),
        in_specs=[fspec, fspec, hspec, hspec],
        out_specs=[ospec, ospec],
        out_shape=[jax.ShapeDtypeStruct((seq_len, hw), F32)] * 2,
        compiler_params=_params(("parallel", "parallel")),
        name="hyena_filter_spectrum",
    )(fc, fs, hs, hd)


def _alt_sign(idx):
    return jnp.where(idx % 2 == 1, -1.0, 1.0)


def _dft_fwd_kernel(ce_ref, so_ref, co_ref, se_ref, e_ref, o_ref, side_ref, ae_ref, de_ref, ao_ref, do_ref,
                    w1e_ref, w2e_ref, w1o_ref, w2o_ref):
    tm = ce_ref.shape[0]
    e, o = e_ref[0], o_ref[0]
    pe_all = jnp.dot(ce_ref[...], e, preferred_element_type=F32)
    qo_all = jnp.dot(so_ref[...], e, preferred_element_type=F32)
    po_all = jnp.dot(co_ref[...], o, preferred_element_type=F32)
    qe_all = jnp.dot(se_ref[...], o, preferred_element_type=F32)
    mid = side_ref[0, 0:1, :]
    nyq = side_ref[0, 1:2, :]
    sub = _pick(tm, EPILOGUE_ROWS, 8)
    for r0 in range(0, tm, sub):
        rs = slice(r0, r0 + sub)
        m = pl.program_id(0) * tm + r0 + lax.broadcasted_iota(jnp.int32, (sub, 1), 0)
        corr = _alt_sign(m) * mid
        row0 = m == 0
        pe, qe, a, d = pe_all[rs] + corr, qe_all[rs], ae_ref[rs, :], de_ref[rs, :]
        qd = jnp.where(row0, nyq, qe) * d
        w1e_ref[0, rs, :] = (pe * a - jnp.where(row0, 0.0, qd)).astype(w1e_ref.dtype)
        w2e_ref[0, rs, :] = jnp.where(row0, qd, pe * d + qe * a).astype(w2e_ref.dtype)
        po, qo, a, d = po_all[rs], qo_all[rs] + corr, ao_ref[rs, :], do_ref[rs, :]
        w1o_ref[0, rs, :] = (po * a - qo * d).astype(w1o_ref.dtype)
        w2o_ref[0, rs, :] = (po * d + qo * a).astype(w2o_ref.dtype)


def _dft_fwd(mats, e, o, side, ka, kd):
    ce, se, co, so, _, _ = mats
    bsz, half, hw = e.shape
    ae, ao, de, do = ka[0::2], ka[1::2], kd[0::2], kd[1::2]
    tm = _pick(half, 512, HALO)
    tn = _pick(hw, 512, LANES)
    fspec = pl.BlockSpec((tm, half), lambda i, j, b: (i, 0))
    uspec = pl.BlockSpec((1, half, tn), lambda i, j, b: (b, 0, j))
    kspec = pl.BlockSpec((tm, tn), lambda i, j, b: (i, j))
    ospec = pl.BlockSpec((1, tm, tn), lambda i, j, b: (b, i, j))
    return pl.pallas_call(
        _dft_fwd_kernel,
        grid=(half // tm, hw // tn, bsz),
        in_specs=[fspec] * 4 + [uspec, uspec, pl.BlockSpec((1, 8, tn), lambda i, j, b: (b, 0, j))] + [kspec] * 4,
        out_specs=[ospec] * 4,
        out_shape=[jax.ShapeDtypeStruct((bsz, half, hw), BF16)] * 4,
        compiler_params=_params(("parallel", "parallel", "parallel")),
        name="hyena_dft_forward",
    )(ce, so, co, se, e, o, side, ae, de, ao, do)


def _dft_inv_kernel(ce_ref, sot_ref, cot_ref, se_ref, alt_ref, w1e_ref, w2e_ref, w1o_ref, w2o_ref,
                    ua_ref, ub_ref, xa_ref, xb_ref, skip_ref, oa_ref, ob_ref):
    tm = oa_ref.shape[1]
    half = w1e_ref.shape[1]
    w1e, w2o = w1e_ref[0], w2o_ref[0]
    s1_all = (jnp.dot(ce_ref[...], w1e, preferred_element_type=F32)
              + jnp.dot(sot_ref[...], w2o, preferred_element_type=F32))
    s2_all = (jnp.dot(cot_ref[...], w1o_ref[0], preferred_element_type=F32)
              + jnp.dot(se_ref[...], w2e_ref[0], preferred_element_type=F32))
    nyq = w2e_ref[0, 0:1, :].astype(F32)
    skip = skip_ref[...]
    sub = _pick(tm, EPILOGUE_ROWS, 8)
    for r0 in range(0, tm, sub):
        rs = slice(r0, r0 + sub)
        t = pl.program_id(0) * tm + r0 + lax.broadcasted_iota(jnp.int32, (sub, 1), 0)
        base = s1_all[rs] + _alt_sign(t) * nyq
        ya = base + s2_all[rs] + ua_ref[0, rs, :].astype(F32) * skip
        yb = base - s2_all[rs] + ub_ref[0, rs, :].astype(F32) * skip
        oa_ref[0, rs, :] = (ya * xa_ref[0, rs, :].astype(F32)).astype(oa_ref.dtype)
        ob_ref[0, rs, :] = (yb * xb_ref[0, rs, :].astype(F32)).astype(ob_ref.dtype)

    @pl.when(pl.program_id(0) == 0)
    def _():
        alt = alt_ref[...]
        mid = (jnp.dot(alt, w1e, preferred_element_type=F32) + jnp.dot(alt, w2o, preferred_element_type=F32))[0:1]
        y = mid + (1.0 if half % 2 == 0 else -1.0) * nyq + ub_ref[0, 0:1, :].astype(F32) * skip
        ob_ref[0, 0:1, :] = (y * xb_ref[0, 0:1, :].astype(F32)).astype(ob_ref.dtype)


def _dft_inv(mats, w1e, w2e, w1o, w2o, ua, ub, xa, xb, skip):
    ce, se, _, _, cot, sot = mats
    bsz, half, hw = ua.shape
    tm = _pick(half, 512, HALO)
    tn = _pick(hw, 512, LANES)
    alt = jnp.zeros((HALO, half), F32).at[0].set(_alt_sign(jnp.arange(half))).astype(BF16)
    fspec = pl.BlockSpec((tm, half), lambda i, j, b: (i, 0))
    wspec = pl.BlockSpec((1, half, tn), lambda i, j, b: (b, 0, j))
    tspec = pl.BlockSpec((1, tm, tn), lambda i, j, b: (b, i, j))
    return pl.pallas_call(
        _dft_inv_kernel,
        grid=(half // tm, hw // tn, bsz),
        in_specs=[fspec] * 4 + [pl.BlockSpec((HALO, half), lambda i, j, b: (0, 0))] + [wspec] * 4 + [tspec] * 4
                 + [pl.BlockSpec((1, tn), lambda i, j, b: (0, j))],
        out_specs=[tspec, tspec],
        out_shape=[jax.ShapeDtypeStruct((bsz, half, hw), BF16)] * 2,
        compiler_params=_params(("parallel", "parallel", "parallel")),
        name="hyena_dft_inverse",
    )(ce, sot, cot, se, alt, w1e, w2e, w1o, w2o, ua, ub, xa, xb, skip.reshape(1, hw).astype(F32))


def _unfold_kernel(oa_ref, ob_ref, o_ref):
    half = oa_ref.shape[1]
    o_ref[0, 0:half, :] = oa_ref[0]
    o_ref[0, half:2 * half, :] = _negate_index(ob_ref[0])


def _unfold(oa, ob):
    bsz, half, hw = oa.shape
    tc = _pick(hw, 512, LANES)
    ispec = pl.BlockSpec((1, half, tc), lambda b, c: (b, 0, c))
    return pl.pallas_call(
        _unfold_kernel,
        grid=(bsz, hw // tc),
        in_specs=[ispec, ispec],
        out_specs=pl.BlockSpec((1, 2 * half, tc), lambda b, c: (b, 0, c)),
        out_shape=jax.ShapeDtypeStruct((bsz, 2 * half, hw), oa.dtype),
        compiler_params=_params(("parallel", "parallel")),
        name="hyena_unfold",
    )(oa, ob)


def _ffn_up_kernel(x_ref, xp_ref, xn_ref, g_ref, wg_ref, wv_ref, cw_ref, cb_ref, o_ref, h_ref, halo_ref,
                   *, seq_len, n_j):
    s = pl.program_id(0)
    tm = x_ref.shape[0]

    @pl.when(s % n_j == 0)
    def _():
        g = g_ref[...]
        _rms_rows_to(x_ref, g_ref, h_ref)
        halo_ref[0:HALO] = _rms_scale(xp_ref[...], g).astype(halo_ref.dtype)
        halo_ref[HALO:2 * HALO] = _rms_scale(xn_ref[...], g).astype(halo_ref.dtype)

    h = h_ref[...]
    wg = wg_ref[...]
    gu = jnp.dot(h, wg, preferred_element_type=F32)
    val = jnp.dot(h, wv_ref[...], preferred_element_type=F32)
    hg = jnp.dot(halo_ref[...], wg, preferred_element_type=F32)
    row_start = (s // n_j) * tm
    above = jnp.where(row_start % seq_len == 0, 0.0, hg[HALO - 8:HALO])
    below = jnp.where((row_start + tm) % seq_len == 0, 0.0, hg[HALO:HALO + 8])
    cw = cw_ref[...]
    cb = cb_ref[...]
    sub = _pick(tm, EPILOGUE_ROWS, 8)
    ext = sub + 16
    for r0 in range(0, tm, sub):
        before = above if r0 == 0 else gu[r0 - 8:r0]
        after = below if r0 + sub == tm else gu[r0 + sub:r0 + sub + 8]
        g_ext = jnp.concatenate([before, gu[r0:r0 + sub], after], axis=0)
        g_m1 = pltpu.roll(g_ext, 1, 0)[8:8 + sub]
        g_p1 = pltpu.roll(g_ext, ext - 1, 0)[8:8 + sub]
        conv = g_m1 * cw[0:1] + g_ext[8:8 + sub] * cw[1:2] + g_p1 * cw[2:3] + cb
        gelu = 0.5 * conv * (1.0 + lax.erf(conv * math.sqrt(0.5)))
        o_ref[r0:r0 + sub, :] = (gelu * val[r0:r0 + sub]).astype(o_ref.dtype)


def _ffn_up(x, gain, w_up, conv_w, conv_b, seq_len):
    rows, d = x.shape
    d_ff = w_up.shape[1] // 2
    tm = _pick(seq_len, 1024, HALO)
    tn = _pick(d_ff, 256, LANES)
    n_i, n_j = rows // tm, d_ff // tn
    hb = tm // HALO
    n_halo = rows // HALO
    return pl.pallas_call(
        functools.partial(_ffn_up_kernel, seq_len=seq_len, n_j=n_j),
        grid=(n_i * n_j,),
        in_specs=[pl.BlockSpec((tm, d), lambda s: (s // n_j, 0), pipeline_mode=pl.Buffered(1)),
                  pl.BlockSpec((HALO, d), lambda s: (jnp.maximum((s // n_j) * hb - 1, 0), 0)),
                  pl.BlockSpec((HALO, d), lambda s: (jnp.minimum((s // n_j + 1) * hb, n_halo - 1), 0)),
                  pl.BlockSpec((1, d), lambda s: (0, 0)),
                  pl.BlockSpec((d, tn), lambda s: (0, s % n_j)),
                  pl.BlockSpec((d, tn), lambda s: (0, n_j + s % n_j)),
                  pl.BlockSpec((3, tn), lambda s: (0, s % n_j)),
                  pl.BlockSpec((1, tn), lambda s: (0, s % n_j))],
        out_specs=pl.BlockSpec((tm, tn), lambda s: (s // n_j, s % n_j)),
        out_shape=jax.ShapeDtypeStruct((rows, d_ff), BF16),
        scratch_shapes=[pltpu.VMEM((tm, d), BF16), pltpu.VMEM((2 * HALO, d), BF16)],
        compiler_params=_params(("arbitrary",)),
        name="convglu_up",
    )(x, x, x, gain.reshape(1, d).astype(F32), w_up, w_up, conv_w.astype(F32),
      conv_b.reshape(1, d_ff).astype(F32))


def _prep_layer(w):
    names = ("w_in", "w_attn_o", "w_hyena_o", "w_out", "w_up", "w_down", "w_ple_gate", "w_ple")
    return {k: w[k].astype(BF16) for k in names}


def _encoder_layer(x, p, tab, w, wp, seq_len):
    rows, d_model = x.shape
    bsz = rows // seq_len
    attn_w = w["w_attn_o"].shape[0]
    hw = w["w_hyena_o"].shape[0]
    in_cols = w["w_in"].shape[1]
    kv_w = (in_cols - attn_w - 3 * hw - 2 * d_model) // 2
    hy_off = attn_w + 2 * kv_w
    gate_off = hy_off + 3 * hw
    tm = _pick(seq_len, 1024, HALO)

    z = _fused_matmul([(x, wp["w_in"])], [], _ep_plain, BF16, in_cols, tm, _pick(in_cols, 1024, LANES),
                      "in_proj", norm_gain=w["g_mix"])
    attn = _attention(z, tab, w["attn_sink"], seq_len, attn_w, kv_w)

    ua, ub, xa, xb, e, o, side = _hy_pre(z.reshape(bsz, seq_len, in_cols), w["hy_short_w"], w["hy_short_b"],
                                         hy_off, hw)
    hs, hd = _filter_mlp(seq_len, w["hy_filt_w1"], w["hy_filt_b1"], w["hy_filt_f1"], w["hy_filt_w2"],
                         w["hy_filt_b2"], w["hy_filt_f2"], w["hy_filt_w3"], w["hy_decay"])
    fc, fs = _dft_mats(seq_len)
    ka, kd = _filter_spectrum(fc, fs, hs, hd)
    mats = _folded_dft_mats(seq_len)
    w1e, w2e, w1o, w2o = _dft_fwd(mats, e, o, side, ka, kd)
    oa, ob = _dft_inv(mats, w1e, w2e, w1o, w2o, ua, ub, xa, xb, w["hy_skip"])
    hy_out = _unfold(oa, ob).reshape(rows, hw)

    tn = _pick(math.gcd(d_model, gate_off), 512, LANES)
    merged = _fused_matmul([(attn, wp["w_attn_o"]), (hy_out, wp["w_hyena_o"])],
                           [(z, "tile", gate_off), (z, "tile", gate_off + d_model)],
                           _ep_branch_merge, BF16, d_model, tm, tn, "branch_merge")
    tn = _pick(d_model, 512, LANES)
    x = _fused_matmul([(merged, wp["w_out"])], [(x, "tile", 0)], _ep_residual, F32, d_model, tm, tn, "out_proj")

    act = _ffn_up(x, w["g_ffn"], wp["w_up"], w["ffn_conv_w"], w["ffn_conv_b"], seq_len)
    x = _fused_matmul([(act, wp["w_down"])], [(x, "tile", 0)], _ep_residual, F32, d_model,
                      _pick(seq_len, 512, HALO), _pick(d_model, 512, LANES), "ffn_down")

    x = _fused_matmul([(x, wp["w_ple_gate"]), (p.astype(BF16), wp["w_ple"])], [(x, "tile", 0)],
                      _ep_ple, F32, d_model, tm, tn, "ple_gate", norm_gain=w["g_ple"])
    return x


_LAYER_WEIGHT_NAMES = ("g_mix", "w_in", "attn_sink", "hy_short_w", "hy_short_b", "hy_filt_w1", "hy_filt_b1",
                       "hy_filt_f1", "hy_filt_w2", "hy_filt_b2", "hy_filt_f2", "hy_filt_w3", "hy_decay",
                       "hy_skip", "w_attn_o", "w_hyena_o", "w_out", "g_ffn", "w_up", "ffn_conv_w", "ffn_conv_b",
                       "w_down", "g_ple", "w_ple_gate", "w_ple")


def _trunk(x, p, tab, layers, preps, g_final):
    bsz, seq_len, d_model = x.shape
    xf = x.reshape(bsz * seq_len, d_model)
    for l, (w, wp) in enumerate(zip(layers, preps)):
        xf = _encoder_layer(xf, p[l].reshape(bsz * seq_len, -1), tab, w, wp, seq_len)
    return _rmsnorm(xf, g_final, x.dtype).reshape(bsz, seq_len, d_model)


def kernel(x_prompt, x_sample, p_prompt, p_sample, rel_bias, g_mix, w_in, attn_sink, hy_short_w, hy_short_b, hy_filt_w1, hy_filt_b1, hy_filt_f1, hy_filt_w2, hy_filt_b2, hy_filt_f2, hy_filt_w3, hy_decay, hy_skip, w_attn_o, w_hyena_o, w_out, g_ffn, w_up, ffn_conv_w, ffn_conv_b, w_down, g_ple, w_ple_gate, w_ple, g_final):
    stacked = (g_mix, w_in, attn_sink, hy_short_w, hy_short_b, hy_filt_w1, hy_filt_b1, hy_filt_f1, hy_filt_w2,
               hy_filt_b2, hy_filt_f2, hy_filt_w3, hy_decay, hy_skip, w_attn_o, w_hyena_o, w_out, g_ffn, w_up,
               ffn_conv_w, ffn_conv_b, w_down, g_ple, w_ple_gate, w_ple)
    depth = g_mix.shape[0]
    layers = [dict(zip(_LAYER_WEIGHT_NAMES, [a[l] for a in stacked])) for l in range(depth)]
    preps = [_prep_layer(w) for w in layers]
    tab = _bias_table(rel_bias)
    y_prompt = _trunk(x_prompt, p_prompt, tab, layers, preps, g_final)
    y_sample = _trunk(x_sample, p_sample, tab, layers, preps, g_final)
    return (y_prompt, y_sample)
```

```python
import functools
import math

import jax
import jax.numpy as jnp
from jax import lax
from jax.experimental import pallas as pl
from jax.experimental.pallas import tpu as pltpu

F32 = jnp.float32
BF16 = jnp.bfloat16

EPS = 1e-6
NEG_INF = -1e30
HEAD_DIM = 128
WINDOW = 128
BLOCK = 128
N_BUCKETS = 32
MAX_DISTANCE = 128
FILTER_BANDS = 16

V7X_VMEM_LIMIT_BYTES = 56 * 1024 * 1024
LANES = 128
HALO = 16
EPILOGUE_ROWS = 32
NORM_ROWS = 32
ATTN_ROWS = 32
ATTN_BLOCKS_PER_STEP = 4
PERM_ROWS = 256


def _pick(n, pref, align):
    t = min(pref, n)
    t -= t % align
    while t >= align:
        if n % t == 0:
            return t
        t -= align
    return n


def _params(sem):
    return pltpu.CompilerParams(dimension_semantics=sem, vmem_limit_bytes=V7X_VMEM_LIMIT_BYTES)


def _rms_scale(x, g):
    ms = jnp.mean(x * x, axis=-1, keepdims=True)
    return x * lax.rsqrt(ms + EPS) * g


def _rms_rows_to(x_ref, g_ref, h_ref):
    rows = x_ref.shape[0]
    chunk = _pick(rows, NORM_ROWS, 8)
    g = g_ref[...]

    def body(c, carry):
        r = pl.ds(pl.multiple_of(c * chunk, chunk), chunk)
        h_ref[r, :] = _rms_scale(x_ref[r, :].astype(F32), g).astype(h_ref.dtype)
        return carry

    n_chunks = rows // chunk
    lax.fori_loop(0, n_chunks, body, 0, unroll=math.gcd(n_chunks, 8))


def _rmsnorm_kernel(x_ref, g_ref, o_ref):
    o_ref[...] = _rms_scale(x_ref[...].astype(F32), g_ref[...]).astype(o_ref.dtype)


def _rmsnorm(x, g, out_dtype):
    rows, d = x.shape
    tm = _pick(rows, 512, 8)
    return pl.pallas_call(
        _rmsnorm_kernel,
        grid=(rows // tm,),
        in_specs=[pl.BlockSpec((tm, d), lambda i: (i, 0)), pl.BlockSpec((1, d), lambda i: (0, 0))],
        out_specs=pl.BlockSpec((tm, d), lambda i: (i, 0)),
        out_shape=jax.ShapeDtypeStruct((rows, d), out_dtype),
        compiler_params=_params(("parallel",)),
        name="rmsnorm",
    )(x, g.reshape(1, d).astype(F32))


def _fused_matmul_kernel(*refs, n_pairs, n_extras, epilogue, norm, n_j):
    s = pl.program_id(0)
    pos = 0
    a_refs, b_refs = [], []
    for p in range(n_pairs):
        a_refs.append(refs[pos])
        pos += 1
        if norm and p == 0:
            g_ref = refs[pos]
            pos += 1
        b_refs.append(refs[pos])
        pos += 1
    extra_refs = refs[pos:pos + n_extras]
    o_ref = refs[pos + n_extras]
    if norm:
        h_ref = refs[pos + n_extras + 1]

        @pl.when(s % n_j == 0)
        def _():
            _rms_rows_to(a_refs[0], g_ref, h_ref)

        a_refs[0] = h_ref

    dots = [jnp.dot(a_refs[p][...], b_refs[p][...], preferred_element_type=F32) for p in range(n_pairs)]
    tm = o_ref.shape[0]
    sub = _pick(tm, EPILOGUE_ROWS, 8)
    for r0 in range(0, tm, sub):
        rs = slice(r0, r0 + sub)
        extras = [r[rs, :] if r.shape[0] == tm else r[...] for r in extra_refs]
        o_ref[rs, :] = epilogue([d[rs] for d in dots], extras).astype(o_ref.dtype)


def _fused_matmul(pairs, extras, epilogue, out_dtype, n_cols, tm, tn, name, norm_gain=None):
    rows = pairs[0][0].shape[0]
    assert rows % tm == 0 and n_cols % tn == 0
    n_i, n_j = rows // tm, n_cols // tn
    norm = norm_gain is not None
    args, in_specs, scratch = [], [], []
    for p, (a, b) in enumerate(pairs):
        k = a.shape[1]
        assert b.shape[0] == k
        if norm and p == 0:
            args += [a, norm_gain.reshape(1, k).astype(F32), b]
            in_specs += [pl.BlockSpec((tm, k), lambda s: (s // n_j, 0), pipeline_mode=pl.Buffered(1)),
                         pl.BlockSpec((1, k), lambda s: (0, 0))]
            scratch.append(pltpu.VMEM((tm, k), BF16))
        else:
            args += [a, b]
            in_specs.append(pl.BlockSpec((tm, k), lambda s: (s // n_j, 0)))
        in_specs.append(pl.BlockSpec((k, tn), lambda s: (0, s % n_j)))
    for arr, kind, off in extras:
        assert off % tn == 0
        ob = off // tn
        args.append(arr)
        if kind == "tile":
            in_specs.append(pl.BlockSpec((tm, tn), lambda s, ob=ob: (s // n_j, s % n_j + ob)))
        else:
            in_specs.append(pl.BlockSpec((1, tn), lambda s, ob=ob: (0, s % n_j + ob)))
    return pl.pallas_call(
        functools.partial(_fused_matmul_kernel, n_pairs=len(pairs), n_extras=len(extras), epilogue=epilogue,
                          norm=norm, n_j=n_j),
        grid=(n_i * n_j,),
        in_specs=in_specs,
        out_specs=pl.BlockSpec((tm, tn), lambda s: (s // n_j, s % n_j)),
        out_shape=jax.ShapeDtypeStruct((rows, n_cols), out_dtype),
        scratch_shapes=scratch,
        compiler_params=_params(("arbitrary",)),
        name=name,
    )(*args)


def _ep_plain(dots, extras):
    return dots[0]


def _sigmoid(x):
    return 0.5 * (1.0 + jnp.tanh(0.5 * x))


def _ep_branch_merge(dots, extras):
    ga, gh = extras
    return _sigmoid(ga.astype(F32)) * dots[0] + _sigmoid(gh.astype(F32)) * dots[1]


def _ep_residual(dots, extras):
    return extras[0] + dots[0]


def _ep_ple(dots, extras):
    return extras[0] + _sigmoid(dots[0]) * dots[1]


def _bias_table_kernel(rb_ref, bkt_ref, o_ref):
    h = pl.program_id(0)
    bkt = bkt_ref[...]
    q = lax.broadcasted_iota(jnp.int32, bkt.shape, 0)
    s = lax.broadcasted_iota(jnp.int32, bkt.shape, 1)
    rel = s - BLOCK - q
    acc = jnp.zeros(bkt.shape, F32)
    for b in range(N_BUCKETS):
        acc = jnp.where(bkt == b, rb_ref[b, h], acc)
    o_ref[0] = jnp.where(jnp.abs(rel) <= WINDOW, acc, NEG_INF)


def _bias_table(rel_bias):
    n_heads = rel_bias.shape[1]
    half = N_BUCKETS // 2
    max_exact = half // 2
    qi = jnp.arange(BLOCK)[:, None]
    sj = jnp.arange(3 * BLOCK)[None, :]
    rel = sj - BLOCK - qi
    ret = jnp.where(rel > 0, half, 0)
    n = jnp.abs(rel)
    nf = jnp.maximum(n, 1).astype(F32)
    large = max_exact + (jnp.log(nf / max_exact) / math.log(MAX_DISTANCE / max_exact)
                         * (half - max_exact)).astype(jnp.int32)
    large = jnp.minimum(large, half - 1)
    bucket = (ret + jnp.where(n < max_exact, n, large)).astype(jnp.int32)
    return pl.pallas_call(
        _bias_table_kernel,
        grid=(n_heads,),
        in_specs=[pl.BlockSpec(memory_space=pltpu.SMEM),
                  pl.BlockSpec((BLOCK, 3 * BLOCK), lambda h: (0, 0))],
        out_specs=pl.BlockSpec((1, BLOCK, 3 * BLOCK), lambda h: (h, 0, 0)),
        out_shape=jax.ShapeDtypeStruct((n_heads, BLOCK, 3 * BLOCK), F32),
        compiler_params=_params(("arbitrary",)),
        name="attn_bias_table",
    )(rel_bias.astype(F32), bucket)


def _attn_kernel(sink_ref, q_ref, kp_ref, km_ref, kn_ref, vp_ref, vm_ref, vn_ref, tab_ref, o_ref, p_ref,
                 *, nb, nq, n_kv, q_per_kv):
    first = (pl.program_id(0) * nq) % nb
    col = lax.broadcasted_iota(jnp.int32, (1, 3 * BLOCK), 1)
    scale = HEAD_DIM ** -0.5
    n_rows = q_per_kv * BLOCK

    def band(prev_ref, main_ref, next_ref, j, ks):
        parts = []
        for b in (j - 1, j, j + 1):
            if b < 0:
                parts.append(prev_ref[:, ks])
            elif b >= nq:
                parts.append(next_ref[:, ks])
            else:
                parts.append(main_ref[b * BLOCK:(b + 1) * BLOCK, ks])
        return jnp.concatenate(parts, axis=0)

    for j in range(nq):
        n = first + j
        outside = ((col < BLOCK) & (n == 0)) | ((col >= 2 * BLOCK) & (n == nb - 1))
        qrows = slice(j * BLOCK, (j + 1) * BLOCK)
        for kh in range(n_kv):
            ks = slice(kh * HEAD_DIM, (kh + 1) * HEAD_DIM)
            kband = band(kp_ref, km_ref, kn_ref, j, ks)
            vband = band(vp_ref, vm_ref, vn_ref, j, ks)
            heads = list(range(kh * q_per_kv, (kh + 1) * q_per_kv))
            qh = jnp.concatenate([q_ref[qrows, h * HEAD_DIM:(h + 1) * HEAD_DIM] for h in heads], axis=0)
            s_all = lax.dot_general(qh, kband, (((1,), (1,)), ((), ())), preferred_element_type=F32)
            inv = []
            for c in range(0, n_rows, ATTN_ROWS):
                h = heads[c // BLOCK]
                s = s_all[c:c + ATTN_ROWS] * scale + tab_ref[h, c % BLOCK:c % BLOCK + ATTN_ROWS, :]
                s = jnp.where(outside, NEG_INF, s)
                sink = sink_ref[h]
                m = jnp.maximum(jnp.max(s, axis=-1, keepdims=True), sink)
                p = jnp.exp(s - m)
                denom = jnp.sum(p, axis=-1, keepdims=True) + jnp.exp(sink - m)
                p_ref[c:c + ATTN_ROWS, :] = p.astype(p_ref.dtype)
                inv.append(denom)
            o = jnp.dot(p_ref[...], vband, preferred_element_type=F32) / jnp.concatenate(inv, axis=0)
            for g, h in enumerate(heads):
                o_ref[qrows, h * HEAD_DIM:(h + 1) * HEAD_DIM] = o[g * BLOCK:(g + 1) * BLOCK].astype(o_ref.dtype)


def _attention(z, tab, sink, seq_len, attn_w, kv_w):
    rows = z.shape[0]
    nb = seq_len // BLOCK
    nq = math.gcd(nb, ATTN_BLOCKS_PER_STEP)
    nblk = rows // BLOCK
    n_kv = kv_w // HEAD_DIM
    q_per_kv = attn_w // kv_w
    assert attn_w % kv_w == 0
    kcol = attn_w // kv_w
    vcol = kcol + 1

    def prev(c):
        return lambda r: (jnp.maximum(r * nq - 1, 0), c)

    def main(c):
        return lambda r: (r, c)

    def nxt(c):
        return lambda r: (jnp.minimum((r + 1) * nq, nblk - 1), c)

    side_spec = lambda f, c: pl.BlockSpec((BLOCK, kv_w), f(c))
    main_spec = lambda c: pl.BlockSpec((nq * BLOCK, kv_w), main(c))
    return pl.pallas_call(
        functools.partial(_attn_kernel, nb=nb, nq=nq, n_kv=n_kv, q_per_kv=q_per_kv),
        grid=(nblk // nq,),
        in_specs=[pl.BlockSpec(memory_space=pltpu.SMEM),
                  pl.BlockSpec((nq * BLOCK, attn_w), lambda r: (r, 0)),
                  side_spec(prev, kcol), main_spec(kcol), side_spec(nxt, kcol),
                  side_spec(prev, vcol), main_spec(vcol), side_spec(nxt, vcol),
                  pl.BlockSpec(tab.shape, lambda r: (0, 0, 0))],
        out_specs=pl.BlockSpec((nq * BLOCK, attn_w), lambda r: (r, 0)),
        out_shape=jax.ShapeDtypeStruct((rows, attn_w), BF16),
        scratch_shapes=[pltpu.VMEM((q_per_kv * BLOCK, 3 * BLOCK), BF16)],
        compiler_params=_params(("parallel",)),
        name="windowed_gqa",
    )(sink.astype(F32), z, z, z, z, z, z, z, tab)


def _conv3_full(x, w, b, rows, n_rows):
    xm = jnp.where(rows == 0, 0.0, pltpu.roll(x, 1, 0))
    xp = jnp.where(rows == n_rows - 1, 0.0, pltpu.roll(x, n_rows - 1, 0))
    return xm * w[0:1] + x * w[1:2] + xp * w[2:3] + b


def _negate_index(x):
    n, _ = x.shape
    rb = min(n, PERM_ROWS)
    nbk = n // rb
    r = lax.broadcasted_iota(jnp.int32, (rb, rb), 0)
    c = lax.broadcasted_iota(jnp.int32, (rb, rb), 1)
    flip = (r + c == rb).astype(x.dtype)
    first = lax.broadcasted_iota(jnp.int32, (rb, 1), 0) == 0
    outs = []
    for bj in range(nbk):
        a = x[(nbk - 1 - bj) * rb:(nbk - bj) * rb]
        b0 = ((nbk - bj) % nbk) * rb
        out = jnp.dot(flip, a, preferred_element_type=F32).astype(x.dtype)
        outs.append(jnp.where(first, x[b0:b0 + 1], out))
    return jnp.concatenate(outs, axis=0)


def _hy_pre_kernel(v_ref, x1_ref, x0_ref, wv_ref, w1_ref, w0_ref, bv_ref, b1_ref, b0_ref,
                   ua_ref, ub_ref, xa_ref, xb_ref, e_ref, o_ref, side_ref):
    n_rows = v_ref.shape[1]
    half = n_rows // 2
    rows = lax.broadcasted_iota(jnp.int32, (n_rows, 1), 0)
    hv = _conv3_full(v_ref[0].astype(F32), wv_ref[...], bv_ref[...], rows, n_rows)
    hx1 = _conv3_full(x1_ref[0].astype(F32), w1_ref[...], b1_ref[...], rows, n_rows)
    u = (hv * hx1).astype(ua_ref.dtype)
    hx0 = _conv3_full(x0_ref[0].astype(F32), w0_ref[...], b0_ref[...], rows, n_rows).astype(xa_ref.dtype)
    ua_ref[0] = u[0:half]
    xa_ref[0] = hx0[0:half]
    ub = _negate_index(u[half:n_rows])
    ub_ref[0] = ub
    xb_ref[0] = _negate_index(hx0[half:n_rows])
    ua32 = u[0:half].astype(F32)
    ub32 = ub.astype(F32)
    first = rows[0:half] == 0
    e_ref[0] = jnp.where(first, ua32, ua32 + ub32).astype(e_ref.dtype)
    o_ref[0] = jnp.where(first, ua32, ua32 - ub32).astype(o_ref.dtype)
    u32 = u.astype(F32)
    nyq = jnp.sum(jnp.where(rows % 2 == 1, -u32, u32), axis=0, keepdims=True)
    side_ref[0] = jnp.concatenate([u32[half:half + 1], nyq, jnp.zeros((6, u32.shape[1]), F32)], axis=0)


def _hy_pre(z3, short_w, short_b, hy_off, hw):
    bsz, seq_len, _ = z3.shape
    half = seq_len // 2
    tc = LANES
    assert hy_off % tc == 0 and hw % tc == 0
    ob, nc = hy_off // tc, hw // tc
    zspec = lambda part: pl.BlockSpec((1, seq_len, tc), lambda b, c: (b, 0, ob + part * nc + c))
    wspec = lambda part: pl.BlockSpec((3, tc), lambda b, c: (0, part * nc + c))
    bspec = lambda part: pl.BlockSpec((1, tc), lambda b, c: (0, part * nc + c))
    ospec = pl.BlockSpec((1, half, tc), lambda b, c: (b, 0, c))
    sw = short_w.astype(F32)
    sb = short_b.reshape(1, -1).astype(F32)
    return pl.pallas_call(
        _hy_pre_kernel,
        grid=(bsz, nc),
        in_specs=[zspec(0), zspec(1), zspec(2), wspec(0), wspec(1), wspec(2), bspec(0), bspec(1), bspec(2)],
        out_specs=[ospec] * 6 + [pl.BlockSpec((1, 8, tc), lambda b, c: (b, 0, c))],
        out_shape=[jax.ShapeDtypeStruct((bsz, half, hw), BF16)] * 6 + [jax.ShapeDtypeStruct((bsz, 8, hw), F32)],
        compiler_params=_params(("parallel", "parallel")),
        name="hyena_short_conv",
    )(z3, z3, z3, sw, sw, sw, sb, sb, sb)


def _filter_mlp_kernel(zf_ref, w1_ref, b1_ref, f1_ref, w2_ref, b2_ref, f2_ref, w3f_ref, w3b_ref,
                       decf_ref, decb_ref, hs_ref, hd_ref):
    tl = zf_ref.shape[0]
    hp = lax.Precision.HIGHEST
    zf = zf_ref[...]
    h = jnp.sin(f1_ref[...] * (jnp.dot(zf, w1_ref[...], precision=hp, preferred_element_type=F32) + b1_ref[...]))
    h = jnp.sin(f2_ref[...] * (jnp.dot(h, w2_ref[...], precision=hp, preferred_element_type=F32) + b2_ref[...]))
    t = zf[:, 0:1]
    fwd = jnp.dot(h, w3f_ref[...], precision=hp, preferred_element_type=F32) * jnp.exp(-t * decf_ref[...])
    bwd = jnp.dot(h, w3b_ref[...], precision=hp, preferred_element_type=F32) * jnp.exp(-t * decb_ref[...])
    pos = pl.program_id(0) * tl + lax.broadcasted_iota(jnp.int32, (tl, 1), 0)
    bwd = jnp.where(pos == 0, 0.0, bwd)
    hs_ref[...] = (fwd + bwd).astype(hs_ref.dtype)
    hd_ref[...] = (fwd - bwd).astype(hd_ref.dtype)


def _pad2(a, rows, cols):
    return jnp.pad(a.astype(F32), ((0, rows - a.shape[0]), (0, cols - a.shape[1])))


def _filter_mlp(seq_len, w1, b1, f1, w2, b2, f2, w3, decay):
    hw = decay.shape[1]
    pos = jnp.arange(seq_len, dtype=F32)
    t = pos / (seq_len - 1)
    bands = jnp.linspace(1e-4, FILTER_BANDS - 1, FILTER_BANDS, dtype=F32)
    ang = (2.0 * math.pi / seq_len) * pos[:, None] * bands[None, :]
    zf = jnp.concatenate([t[:, None], jnp.cos(ang), -jnp.sin(ang)], axis=-1)
    emb, hid = w1.shape
    assert emb <= LANES and hid <= LANES
    zf = _pad2(zf, seq_len, LANES)
    row = lambda v: _pad2(v.reshape(1, -1), 1, LANES)
    w3p = _pad2(w3, LANES, 2 * hw)
    dec = decay.reshape(1, 2 * hw).astype(F32)
    tl = _pick(seq_len, 512, 8)
    tc = _pick(hw, 2048, LANES)
    nc = hw // tc
    small = lambda shape: pl.BlockSpec(shape, lambda i, c: (0, 0))
    return pl.pallas_call(
        _filter_mlp_kernel,
        grid=(seq_len // tl, nc),
        in_specs=[pl.BlockSpec((tl, LANES), lambda i, c: (i, 0)),
                  small((LANES, LANES)), small((1, LANES)), small((1, LANES)),
                  small((LANES, LANES)), small((1, LANES)), small((1, LANES)),
                  pl.BlockSpec((LANES, tc), lambda i, c: (0, c)),
                  pl.BlockSpec((LANES, tc), lambda i, c: (0, nc + c)),
                  pl.BlockSpec((1, tc), lambda i, c: (0, c)),
                  pl.BlockSpec((1, tc), lambda i, c: (0, nc + c))],
        out_specs=[pl.BlockSpec((tl, tc), lambda i, c: (i, c))] * 2,
        out_shape=[jax.ShapeDtypeStruct((seq_len, hw), BF16)] * 2,
        compiler_params=_params(("parallel", "parallel")),
        name="hyena_filter_mlp",
    )(zf, _pad2(w1, LANES, LANES), row(b1), row(f1), _pad2(w2, LANES, LANES), row(b2), row(f2),
      w3p, w3p, dec, dec)


def _trig_kernel(cb_ref, sb_ref, ck_ref, sk_ref, c_ref, s_ref, *, nyquist_row):
    tk, n_cols = cb_ref.shape
    cb, sb = cb_ref[...], sb_ref[...]
    ck, sk = ck_ref[0], sk_ref[0]
    sn = sk * cb + ck * sb
    if nyquist_row:
        k = pl.program_id(0) * tk + lax.broadcasted_iota(jnp.int32, (tk, 1), 0)
        n = lax.broadcasted_iota(jnp.int32, (1, n_cols), 1)
        sn = jnp.where(k == 0, jnp.where(n % 2 == 1, -1.0, 1.0), sn)
    c_ref[...] = (ck * cb - sk * sb).astype(c_ref.dtype)
    s_ref[...] = sn.astype(s_ref.dtype)


def _trig_matrices(n, n2, base_idx, tile_idx, name, n_cols=None, nyquist_row=False):
    tk = _pick(n, 256, HALO)
    n_i = n // tk
    rows, n = n, n_cols or n
    col = jnp.arange(n, dtype=jnp.int32)[None, :]

    def tables(idx):
        ang = (idx % n2).astype(F32) * (2.0 * math.pi / n2)
        return jnp.cos(ang), jnp.sin(ang)

    cb, sb = tables(base_idx(jnp.arange(tk, dtype=jnp.int32)[:, None], col))
    ck, sk = tables(tile_idx(jnp.arange(n_i, dtype=jnp.int32)[:, None] * tk, col))
    base = pl.BlockSpec((tk, n), lambda i: (0, 0))
    rowt = pl.BlockSpec((1, 1, n), lambda i: (i, 0, 0))
    ospec = pl.BlockSpec((tk, n), lambda i: (i, 0))
    return pl.pallas_call(
        functools.partial(_trig_kernel, nyquist_row=nyquist_row),
        grid=(n_i,),
        in_specs=[base, base, rowt, rowt],
        out_specs=[ospec] * 2,
        out_shape=[jax.ShapeDtypeStruct((rows, n), BF16)] * 2,
        compiler_params=_params(("parallel",)),
        name=name,
    )(cb, sb, ck.reshape(n_i, 1, n), sk.reshape(n_i, 1, n))


def _spectrum_mats(seq_len):
    half, n2 = seq_len // 2, 2 * seq_len
    even = _trig_matrices(half, n2, lambda d, c: 2 * d * c, lambda r, c: 2 * r * c, "dft_even_rows",
                          n_cols=seq_len, nyquist_row=True)
    odd = _trig_matrices(half, n2, lambda d, c: (2 * d + 1) * c, lambda r, c: 2 * r * c, "dft_odd_rows",
                         n_cols=seq_len)
    return even, odd


def _folded_dft_mats(seq_len):
    half, n2 = seq_len // 2, 2 * seq_len
    ce, se = _trig_matrices(half, n2, lambda d, c: 2 * d * c, lambda r, c: 2 * r * c, "dft_even_bins")
    co, so = _trig_matrices(half, n2, lambda d, c: (2 * d + 1) * c, lambda r, c: 2 * r * c, "dft_odd_bins")
    cot, sot = _trig_matrices(half, n2, lambda d, c: (2 * c + 1) * d, lambda r, c: (2 * c + 1) * r,
                              "dft_odd_bins_t")
    return ce, se, co, so, cot, sot


def _filter_spectrum_kernel(fc_ref, fs_ref, hs_ref, hd_ref, a_ref, d_ref, *, seq_len, even):
    tm = fc_ref.shape[0]
    hs = hs_ref[...]
    scale = 2.0 / (2 * seq_len)
    if even:
        row0 = (pl.program_id(0) * tm + lax.broadcasted_iota(jnp.int32, (tm, 1), 0)) == 0
        scale = jnp.where(row0, 1.0 / (2 * seq_len), scale)
    a_ref[...] = scale * jnp.dot(fc_ref[...], hs, preferred_element_type=F32)
    d_ref[...] = scale * jnp.dot(fs_ref[...], hd_ref[...], preferred_element_type=F32)
    if even:
        @pl.when(pl.program_id(0) == 0)
        def _():
            nyq = jnp.dot(fs_ref[0:HALO, :], hs, preferred_element_type=F32)[0:1]
            d_ref[0:1, :] = nyq * (1.0 / (2 * seq_len))


def _filter_spectrum(fc, fs, hs, hd, even):
    seq_len, hw = hs.shape
    half = fc.shape[0]
    tm = _pick(half, 512, HALO)
    tn = _pick(hw, 512, LANES)
    fspec = pl.BlockSpec((tm, seq_len), lambda i, j: (i, 0))
    hspec = pl.BlockSpec((seq_len, tn), lambda i, j: (0, j))
    ospec = pl.BlockSpec((tm, tn), lambda i, j: (i, j))
    return pl.pallas_call(
        functools.partial(_filter_spectrum_kernel, seq_len=seq_len, even=even),
        grid=(half // tm, hw // tn),
        in_specs=[fspec, fspec, hspec, hspec],
        out_specs=[ospec, ospec],
        out_shape=[jax.ShapeDtypeStruct((half, hw), F32)] * 2,
        compiler_params=_params(("parallel", "parallel")),
        name="hyena_filter_spectrum",
    )(fc, fs, hs, hd)


def _alt_sign(idx):
    return jnp.where(idx % 2 == 1, -1.0, 1.0)


def _dft_fwd_kernel(ce_ref, so_ref, co_ref, se_ref, e_ref, o_ref, side_ref, ae_ref, de_ref, ao_ref, do_ref,
                    w1e_ref, w2e_ref, w1o_ref, w2o_ref):
    tm = ce_ref.shape[0]
    e, o = e_ref[0], o_ref[0]
    pe_all = jnp.dot(ce_ref[...], e, preferred_element_type=F32)
    qo_all = jnp.dot(so_ref[...], e, preferred_element_type=F32)
    po_all = jnp.dot(co_ref[...], o, preferred_element_type=F32)
    qe_all = jnp.dot(se_ref[...], o, preferred_element_type=F32)
    mid = side_ref[0, 0:1, :]
    nyq = side_ref[0, 1:2, :]
    sub = _pick(tm, EPILOGUE_ROWS, 8)
    for r0 in range(0, tm, sub):
        rs = slice(r0, r0 + sub)
        m = pl.program_id(0) * tm + r0 + lax.broadcasted_iota(jnp.int32, (sub, 1), 0)
        corr = _alt_sign(m) * mid
        row0 = m == 0
        pe, qe, a, d = pe_all[rs] + corr, qe_all[rs], ae_ref[rs, :], de_ref[rs, :]
        qd = jnp.where(row0, nyq, qe) * d
        w1e_ref[0, rs, :] = (pe * a - jnp.where(row0, 0.0, qd)).astype(w1e_ref.dtype)
        w2e_ref[0, rs, :] = jnp.where(row0, qd, pe * d + qe * a).astype(w2e_ref.dtype)
        po, qo, a, d = po_all[rs], qo_all[rs] + corr, ao_ref[rs, :], do_ref[rs, :]
        w1o_ref[0, rs, :] = (po * a - qo * d).astype(w1o_ref.dtype)
        w2o_ref[0, rs, :] = (po * d + qo * a).astype(w2o_ref.dtype)


def _dft_fwd(mats, e, o, side, ae, de, ao, do):
    ce, se, co, so, _, _ = mats
    bsz, half, hw = e.shape
    tm = _pick(half, 512, HALO)
    tn = _pick(hw, 512, LANES)
    fspec = pl.BlockSpec((tm, half), lambda i, j, b: (i, 0))
    uspec = pl.BlockSpec((1, half, tn), lambda i, j, b: (b, 0, j))
    kspec = pl.BlockSpec((tm, tn), lambda i, j, b: (i, j))
    ospec = pl.BlockSpec((1, tm, tn), lambda i, j, b: (b, i, j))
    return pl.pallas_call(
        _dft_fwd_kernel,
        grid=(half // tm, hw // tn, bsz),
        in_specs=[fspec] * 4 + [uspec, uspec, pl.BlockSpec((1, 8, tn), lambda i, j, b: (b, 0, j))] + [kspec] * 4,
        out_specs=[ospec] * 4,
        out_shape=[jax.ShapeDtypeStruct((bsz, half, hw), BF16)] * 4,
        compiler_params=_params(("parallel", "parallel", "parallel")),
        name="hyena_dft_forward",
    )(ce, so, co, se, e, o, side, ae, de, ao, do)


def _dft_inv_kernel(ce_ref, sot_ref, cot_ref, se_ref, alt_ref, w1e_ref, w2e_ref, w1o_ref, w2o_ref,
                    ua_ref, ub_ref, xa_ref, xb_ref, skip_ref, oa_ref, ob_ref):
    tm = oa_ref.shape[1]
    half = w1e_ref.shape[1]
    w1e, w2o = w1e_ref[0], w2o_ref[0]
    s1_all = (jnp.dot(ce_ref[...], w1e, preferred_element_type=F32)
              + jnp.dot(sot_ref[...], w2o, preferred_element_type=F32))
    s2_all = (jnp.dot(cot_ref[...], w1o_ref[0], preferred_element_type=F32)
              + jnp.dot(se_ref[...], w2e_ref[0], preferred_element_type=F32))
    nyq = w2e_ref[0, 0:1, :].astype(F32)
    skip = skip_ref[...]
    sub = _pick(tm, EPILOGUE_ROWS, 8)
    for r0 in range(0, tm, sub):
        rs = slice(r0, r0 + sub)
        t = pl.program_id(0) * tm + r0 + lax.broadcasted_iota(jnp.int32, (sub, 1), 0)
        base = s1_all[rs] + _alt_sign(t) * nyq
        ya = base + s2_all[rs] + ua_ref[0, rs, :].astype(F32) * skip
        yb = base - s2_all[rs] + ub_ref[0, rs, :].astype(F32) * skip
        oa_ref[0, rs, :] = (ya * xa_ref[0, rs, :].astype(F32)).astype(oa_ref.dtype)
        ob_ref[0, rs, :] = (yb * xb_ref[0, rs, :].astype(F32)).astype(ob_ref.dtype)

    @pl.when(pl.program_id(0) == 0)
    def _():
        alt = alt_ref[...]
        mid = (jnp.dot(alt, w1e, preferred_element_type=F32) + jnp.dot(alt, w2o, preferred_element_type=F32))[0:1]
        y = mid + (1.0 if half % 2 == 0 else -1.0) * nyq + ub_ref[0, 0:1, :].astype(F32) * skip
        ob_ref[0, 0:1, :] = (y * xb_ref[0, 0:1, :].astype(F32)).astype(ob_ref.dtype)


def _dft_inv(mats, w1e, w2e, w1o, w2o, ua, ub, xa, xb, skip):
    ce, se, _, _, cot, sot = mats
    bsz, half, hw = ua.shape
    tm = _pick(half, 512, HALO)
    tn = _pick(hw, 512, LANES)
    alt = jnp.zeros((HALO, half), F32).at[0].set(_alt_sign(jnp.arange(half))).astype(BF16)
    fspec = pl.BlockSpec((tm, half), lambda i, j, b: (i, 0))
    wspec = pl.BlockSpec((1, half, tn), lambda i, j, b: (b, 0, j))
    tspec = pl.BlockSpec((1, tm, tn), lambda i, j, b: (b, i, j))
    return pl.pallas_call(
        _dft_inv_kernel,
        grid=(half // tm, hw // tn, bsz),
        in_specs=[fspec] * 4 + [pl.BlockSpec((HALO, half), lambda i, j, b: (0, 0))] + [wspec] * 4 + [tspec] * 4
                 + [pl.BlockSpec((1, tn), lambda i, j, b: (0, j))],
        out_specs=[tspec, tspec],
        out_shape=[jax.ShapeDtypeStruct((bsz, half, hw), BF16)] * 2,
        compiler_params=_params(("parallel", "parallel", "parallel")),
        name="hyena_dft_inverse",
    )(ce, sot, cot, se, alt, w1e, w2e, w1o, w2o, ua, ub, xa, xb, skip.reshape(1, hw).astype(F32))


def _unfold_kernel(oa_ref, ob_ref, o_ref):
    half = oa_ref.shape[1]
    o_ref[0, 0:half, :] = oa_ref[0]
    o_ref[0, half:2 * half, :] = _negate_index(ob_ref[0])


def _unfold(oa, ob):
    bsz, half, hw = oa.shape
    tc = _pick(hw, 512, LANES)
    ispec = pl.BlockSpec((1, half, tc), lambda b, c: (b, 0, c))
    return pl.pallas_call(
        _unfold_kernel,
        grid=(bsz, hw // tc),
        in_specs=[ispec, ispec],
        out_specs=pl.BlockSpec((1, 2 * half, tc), lambda b, c: (b, 0, c)),
        out_shape=jax.ShapeDtypeStruct((bsz, 2 * half, hw), oa.dtype),
        compiler_params=_params(("parallel", "parallel")),
        name="hyena_unfold",
    )(oa, ob)


def _ffn_up_kernel(x_ref, xp_ref, xn_ref, g_ref, wg_ref, wv_ref, cw_ref, cb_ref, o_ref, h_ref, halo_ref,
                   *, seq_len, n_j):
    s = pl.program_id(0)
    tm = x_ref.shape[0]

    @pl.when(s % n_j == 0)
    def _():
        g = g_ref[...]
        _rms_rows_to(x_ref, g_ref, h_ref)
        halo_ref[0:HALO] = _rms_scale(xp_ref[...], g).astype(halo_ref.dtype)
        halo_ref[HALO:2 * HALO] = _rms_scale(xn_ref[...], g).astype(halo_ref.dtype)

    h = h_ref[...]
    wg = wg_ref[...]
    gu = jnp.dot(h, wg, preferred_element_type=F32)
    val = jnp.dot(h, wv_ref[...], preferred_element_type=F32)
    hg = jnp.dot(halo_ref[...], wg, preferred_element_type=F32)
    row_start = (s // n_j) * tm
    above = jnp.where(row_start % seq_len == 0, 0.0, hg[HALO - 8:HALO])
    below = jnp.where((row_start + tm) % seq_len == 0, 0.0, hg[HALO:HALO + 8])
    cw = cw_ref[...]
    cb = cb_ref[...]
    sub = _pick(tm, EPILOGUE_ROWS, 8)
    ext = sub + 16
    for r0 in range(0, tm, sub):
        before = above if r0 == 0 else gu[r0 - 8:r0]
        after = below if r0 + sub == tm else gu[r0 + sub:r0 + sub + 8]
        g_ext = jnp.concatenate([before, gu[r0:r0 + sub], after], axis=0)
        g_m1 = pltpu.roll(g_ext, 1, 0)[8:8 + sub]
        g_p1 = pltpu.roll(g_ext, ext - 1, 0)[8:8 + sub]
        conv = g_m1 * cw[0:1] + g_ext[8:8 + sub] * cw[1:2] + g_p1 * cw[2:3] + cb
        gelu = 0.5 * conv * (1.0 + lax.erf(conv * math.sqrt(0.5)))
        o_ref[r0:r0 + sub, :] = (gelu * val[r0:r0 + sub]).astype(o_ref.dtype)


def _ffn_up(x, gain, w_up, conv_w, conv_b, seq_len):
    rows, d = x.shape
    d_ff = w_up.shape[1] // 2
    tm = _pick(seq_len, 1024, HALO)
    tn = _pick(d_ff, 256, LANES)
    n_i, n_j = rows // tm, d_ff // tn
    hb = tm // HALO
    n_halo = rows // HALO
    return pl.pallas_call(
        functools.partial(_ffn_up_kernel, seq_len=seq_len, n_j=n_j),
        grid=(n_i * n_j,),
        in_specs=[pl.BlockSpec((tm, d), lambda s: (s // n_j, 0), pipeline_mode=pl.Buffered(1)),
                  pl.BlockSpec((HALO, d), lambda s: (jnp.maximum((s // n_j) * hb - 1, 0), 0)),
                  pl.BlockSpec((HALO, d), lambda s: (jnp.minimum((s // n_j + 1) * hb, n_halo - 1), 0)),
                  pl.BlockSpec((1, d), lambda s: (0, 0)),
                  pl.BlockSpec((d, tn), lambda s: (0, s % n_j)),
                  pl.BlockSpec((d, tn), lambda s: (0, n_j + s % n_j)),
                  pl.BlockSpec((3, tn), lambda s: (0, s % n_j)),
                  pl.BlockSpec((1, tn), lambda s: (0, s % n_j))],
        out_specs=pl.BlockSpec((tm, tn), lambda s: (s // n_j, s % n_j)),
        out_shape=jax.ShapeDtypeStruct((rows, d_ff), BF16),
        scratch_shapes=[pltpu.VMEM((tm, d), BF16), pltpu.VMEM((2 * HALO, d), BF16)],
        compiler_params=_params(("arbitrary",)),
        name="convglu_up",
    )(x, x, x, gain.reshape(1, d).astype(F32), w_up, w_up, conv_w.astype(F32),
      conv_b.reshape(1, d_ff).astype(F32))


def _prep_layer(w):
    names = ("w_in", "w_attn_o", "w_hyena_o", "w_out", "w_up", "w_down", "w_ple_gate", "w_ple")
    return {k: w[k].astype(BF16) for k in names}


def _encoder_layer(x, p, tab, w, wp, seq_len):
    rows, d_model = x.shape
    bsz = rows // seq_len
    attn_w = w["w_attn_o"].shape[0]
    hw = w["w_hyena_o"].shape[0]
    in_cols = w["w_in"].shape[1]
    kv_w = (in_cols - attn_w - 3 * hw - 2 * d_model) // 2
    hy_off = attn_w + 2 * kv_w
    gate_off = hy_off + 3 * hw
    tm = _pick(seq_len, 1024, HALO)

    z = _fused_matmul([(x, wp["w_in"])], [], _ep_plain, BF16, in_cols, tm, _pick(in_cols, 1024, LANES),
                      "in_proj", norm_gain=w["g_mix"])
    attn = _attention(z, tab, w["attn_sink"], seq_len, attn_w, kv_w)

    ua, ub, xa, xb, e, o, side = _hy_pre(z.reshape(bsz, seq_len, in_cols), w["hy_short_w"], w["hy_short_b"],
                                         hy_off, hw)
    hs, hd = _filter_mlp(seq_len, w["hy_filt_w1"], w["hy_filt_b1"], w["hy_filt_f1"], w["hy_filt_w2"],
                         w["hy_filt_b2"], w["hy_filt_f2"], w["hy_filt_w3"], w["hy_decay"])
    (fce, fse), (fco, fso) = _spectrum_mats(seq_len)
    ae, de = _filter_spectrum(fce, fse, hs, hd, even=True)
    ao, do = _filter_spectrum(fco, fso, hs, hd, even=False)
    mats = _folded_dft_mats(seq_len)
    w1e, w2e, w1o, w2o = _dft_fwd(mats, e, o, side, ae, de, ao, do)
    oa, ob = _dft_inv(mats, w1e, w2e, w1o, w2o, ua, ub, xa, xb, w["hy_skip"])
    hy_out = _unfold(oa, ob).reshape(rows, hw)

    tn = _pick(math.gcd(d_model, gate_off), 512, LANES)
    merged = _fused_matmul([(attn, wp["w_attn_o"]), (hy_out, wp["w_hyena_o"])],
                           [(z, "tile", gate_off), (z, "tile", gate_off + d_model)],
                           _ep_branch_merge, BF16, d_model, tm, tn, "branch_merge")
    tn = _pick(d_model, 512, LANES)
    x = _fused_matmul([(merged, wp["w_out"])], [(x, "tile", 0)], _ep_residual, F32, d_model, tm, tn, "out_proj")

    act = _ffn_up(x, w["g_ffn"], wp["w_up"], w["ffn_conv_w"], w["ffn_conv_b"], seq_len)
    x = _fused_matmul([(act, wp["w_down"])], [(x, "tile", 0)], _ep_residual, F32, d_model,
                      _pick(seq_len, 512, HALO), _pick(d_model, 512, LANES), "ffn_down")

    x = _fused_matmul([(x, wp["w_ple_gate"]), (p.astype(BF16), wp["w_ple"])], [(x, "tile", 0)],
                      _ep_ple, F32, d_model, tm, tn, "ple_gate", norm_gain=w["g_ple"])
    return x


_LAYER_WEIGHT_NAMES = ("g_mix", "w_in", "attn_sink", "hy_short_w", "hy_short_b", "hy_filt_w1", "hy_filt_b1",
                       "hy_filt_f1", "hy_filt_w2", "hy_filt_b2", "hy_filt_f2", "hy_filt_w3", "hy_decay",
                       "hy_skip", "w_attn_o", "w_hyena_o", "w_out", "g_ffn", "w_up", "ffn_conv_w", "ffn_conv_b",
                       "w_down", "g_ple", "w_ple_gate", "w_ple")


def _trunk(x, p, tab, layers, preps, g_final):
    bsz, seq_len, d_model = x.shape
    xf = x.reshape(bsz * seq_len, d_model)
    for l, (w, wp) in enumerate(zip(layers, preps)):
        xf = _encoder_layer(xf, p[l].reshape(bsz * seq_len, -1), tab, w, wp, seq_len)
    return _rmsnorm(xf, g_final, x.dtype).reshape(bsz, seq_len, d_model)


def kernel(x_prompt, x_sample, p_prompt, p_sample, rel_bias, g_mix, w_in, attn_sink, hy_short_w, hy_short_b, hy_filt_w1, hy_filt_b1, hy_filt_f1, hy_filt_w2, hy_filt_b2, hy_filt_f2, hy_filt_w3, hy_decay, hy_skip, w_attn_o, w_hyena_o, w_out, g_ffn, w_up, ffn_conv_w, ffn_conv_b, w_down, g_ple, w_ple_gate, w_ple, g_final):
    stacked = (g_mix, w_in, attn_sink, hy_short_w, hy_short_b, hy_filt_w1, hy_filt_b1, hy_filt_f1, hy_filt_w2,
               hy_filt_b2, hy_filt_f2, hy_filt_w3, hy_decay, hy_skip, w_attn_o, w_hyena_o, w_out, g_ffn, w_up,
               ffn_conv_w, ffn_conv_b, w_down, g_ple, w_ple_gate, w_ple)
    depth = g_mix.shape[0]
    layers = [dict(zip(_LAYER_WEIGHT_NAMES, [a[l] for a in stacked])) for l in range(depth)]
    preps = [_prep_layer(w) for w in layers]
    tab = _bias_table(rel_bias)
    y_prompt = _trunk(x_prompt, p_prompt, tab, layers, preps, g_final)
    y_sample = _trunk(x_sample, p_sample, tab, layers, preps, g_final)
    return (y_prompt, y_sample)
```

```python
import functools
import math

import jax
import jax.numpy as jnp
from jax import lax
from jax.experimental import pallas as pl
from jax.experimental.pallas import tpu as pltpu

F32 = jnp.float32
BF16 = jnp.bfloat16

EPS = 1e-6
NEG_INF = -1e30
HEAD_DIM = 128
WINDOW = 128
BLOCK = 128
N_BUCKETS = 32
MAX_DISTANCE = 128
FILTER_BANDS = 16

V7X_VMEM_LIMIT_BYTES = 56 * 1024 * 1024
LANES = 128
HALO = 16
EPILOGUE_ROWS = 32
NORM_ROWS = 32
ATTN_ROWS = 32
ATTN_BLOCKS_PER_STEP = 4
PERM_ROWS = 256


def _pick(n, pref, align):
    t = min(pref, n)
    t -= t % align
    while t >= align:
        if n % t == 0:
            return t
        t -= align
    return n


def _params(sem):
    return pltpu.CompilerParams(dimension_semantics=sem, vmem_limit_bytes=V7X_VMEM_LIMIT_BYTES)


def _rms_scale(x, g):
    ms = jnp.mean(x * x, axis=-1, keepdims=True)
    return x * lax.rsqrt(ms + EPS) * g


def _rms_rows_to(x_ref, g_ref, h_ref):
    rows = x_ref.shape[0]
    chunk = _pick(rows, NORM_ROWS, 8)
    g = g_ref[...]

    def body(c, carry):
        r = pl.ds(pl.multiple_of(c * chunk, chunk), chunk)
        h_ref[r, :] = _rms_scale(x_ref[r, :].astype(F32), g).astype(h_ref.dtype)
        return carry

    n_chunks = rows // chunk
    lax.fori_loop(0, n_chunks, body, 0, unroll=math.gcd(n_chunks, 8))


def _row_tile_copy(x_hbm, x_buf, sem, i):
    tm = x_buf.shape[0]
    return pltpu.make_async_copy(x_hbm.at[pl.ds(i * tm, tm), :], x_buf, sem)


def _norm_row_tile(x_hbm, x_buf, sem, g_ref, h_ref, s, n_j):
    i = s // n_j
    n_i = x_hbm.shape[0] // x_buf.shape[0]

    @pl.when(s == 0)
    def _():
        _row_tile_copy(x_hbm, x_buf, sem, 0).start()

    @pl.when(s % n_j == 0)
    def _():
        _row_tile_copy(x_hbm, x_buf, sem, i).wait()
        _rms_rows_to(x_buf, g_ref, h_ref)

        @pl.when(i + 1 < n_i)
        def _():
            _row_tile_copy(x_hbm, x_buf, sem, i + 1).start()


def _rmsnorm_kernel(x_ref, g_ref, o_ref):
    o_ref[...] = _rms_scale(x_ref[...].astype(F32), g_ref[...]).astype(o_ref.dtype)


def _rmsnorm(x, g, out_dtype):
    rows, d = x.shape
    tm = _pick(rows, 512, 8)
    return pl.pallas_call(
        _rmsnorm_kernel,
        grid=(rows // tm,),
        in_specs=[pl.BlockSpec((tm, d), lambda i: (i, 0)), pl.BlockSpec((1, d), lambda i: (0, 0))],
        out_specs=pl.BlockSpec((tm, d), lambda i: (i, 0)),
        out_shape=jax.ShapeDtypeStruct((rows, d), out_dtype),
        compiler_params=_params(("parallel",)),
        name="rmsnorm",
    )(x, g.reshape(1, d).astype(F32))


def _fused_matmul_kernel(*refs, n_pairs, n_extras, epilogue, norm, n_j):
    s = pl.program_id(0)
    pos = 0
    a_refs, b_refs = [], []
    for p in range(n_pairs):
        a_refs.append(refs[pos])
        pos += 1
        if norm and p == 0:
            g_ref = refs[pos]
            pos += 1
        b_refs.append(refs[pos])
        pos += 1
    extra_refs = refs[pos:pos + n_extras]
    o_ref = refs[pos + n_extras]
    if norm:
        h_ref, x_buf, x_sem = refs[pos + n_extras + 1:pos + n_extras + 4]
        _norm_row_tile(a_refs[0], x_buf, x_sem, g_ref, h_ref, s, n_j)
        a_refs[0] = h_ref

    dots = [jnp.dot(a_refs[p][...], b_refs[p][...], preferred_element_type=F32) for p in range(n_pairs)]
    tm = o_ref.shape[0]
    sub = _pick(tm, EPILOGUE_ROWS, 8)
    for r0 in range(0, tm, sub):
        rs = slice(r0, r0 + sub)
        extras = [r[rs, :] if r.shape[0] == tm else r[...] for r in extra_refs]
        o_ref[rs, :] = epilogue([d[rs] for d in dots], extras).astype(o_ref.dtype)


def _fused_matmul(pairs, extras, epilogue, out_dtype, n_cols, tm, tn, name, norm_gain=None):
    rows = pairs[0][0].shape[0]
    assert rows % tm == 0 and n_cols % tn == 0
    n_i, n_j = rows // tm, n_cols // tn
    norm = norm_gain is not None
    args, in_specs, scratch = [], [], []
    for p, (a, b) in enumerate(pairs):
        k = a.shape[1]
        assert b.shape[0] == k
        if norm and p == 0:
            args += [a, norm_gain.reshape(1, k).astype(F32), b]
            in_specs += [pl.BlockSpec(memory_space=pl.ANY), pl.BlockSpec((1, k), lambda s: (0, 0))]
            scratch += [pltpu.VMEM((tm, k), BF16), pltpu.VMEM((tm, k), F32), pltpu.SemaphoreType.DMA(())]
        else:
            args += [a, b]
            in_specs.append(pl.BlockSpec((tm, k), lambda s: (s // n_j, 0)))
        in_specs.append(pl.BlockSpec((k, tn), lambda s: (0, s % n_j)))
    for arr, kind, off in extras:
        assert off % tn == 0
        ob = off // tn
        args.append(arr)
        if kind == "tile":
            in_specs.append(pl.BlockSpec((tm, tn), lambda s, ob=ob: (s // n_j, s % n_j + ob)))
        else:
            in_specs.append(pl.BlockSpec((1, tn), lambda s, ob=ob: (0, s % n_j + ob)))
    return pl.pallas_call(
        functools.partial(_fused_matmul_kernel, n_pairs=len(pairs), n_extras=len(extras), epilogue=epilogue,
                          norm=norm, n_j=n_j),
        grid=(n_i * n_j,),
        in_specs=in_specs,
        out_specs=pl.BlockSpec((tm, tn), lambda s: (s // n_j, s % n_j)),
        out_shape=jax.ShapeDtypeStruct((rows, n_cols), out_dtype),
        scratch_shapes=scratch,
        compiler_params=_params(("arbitrary",)),
        name=name,
    )(*args)


def _ep_plain(dots, extras):
    return dots[0]


def _sigmoid(x):
    return 0.5 * (1.0 + jnp.tanh(0.5 * x))


def _ep_branch_merge(dots, extras):
    ga, gh = extras
    return _sigmoid(ga.astype(F32)) * dots[0] + _sigmoid(gh.astype(F32)) * dots[1]


def _ep_residual(dots, extras):
    return extras[0] + dots[0]


def _ep_ple(dots, extras):
    return extras[0] + _sigmoid(dots[0]) * dots[1]


def _bias_table_kernel(rb_ref, bkt_ref, o_ref):
    h = pl.program_id(0)
    bkt = bkt_ref[...]
    q = lax.broadcasted_iota(jnp.int32, bkt.shape, 0)
    s = lax.broadcasted_iota(jnp.int32, bkt.shape, 1)
    rel = s - BLOCK - q
    acc = jnp.zeros(bkt.shape, F32)
    for b in range(N_BUCKETS):
        acc = jnp.where(bkt == b, rb_ref[b, h], acc)
    o_ref[0] = jnp.where(jnp.abs(rel) <= WINDOW, acc, NEG_INF)


def _bias_table(rel_bias):
    n_heads = rel_bias.shape[1]
    half = N_BUCKETS // 2
    max_exact = half // 2
    qi = jnp.arange(BLOCK)[:, None]
    sj = jnp.arange(3 * BLOCK)[None, :]
    rel = sj - BLOCK - qi
    ret = jnp.where(rel > 0, half, 0)
    n = jnp.abs(rel)
    nf = jnp.maximum(n, 1).astype(F32)
    large = max_exact + (jnp.log(nf / max_exact) / math.log(MAX_DISTANCE / max_exact)
                         * (half - max_exact)).astype(jnp.int32)
    large = jnp.minimum(large, half - 1)
    bucket = (ret + jnp.where(n < max_exact, n, large)).astype(jnp.int32)
    return pl.pallas_call(
        _bias_table_kernel,
        grid=(n_heads,),
        in_specs=[pl.BlockSpec(memory_space=pltpu.SMEM),
                  pl.BlockSpec((BLOCK, 3 * BLOCK), lambda h: (0, 0))],
        out_specs=pl.BlockSpec((1, BLOCK, 3 * BLOCK), lambda h: (h, 0, 0)),
        out_shape=jax.ShapeDtypeStruct((n_heads, BLOCK, 3 * BLOCK), F32),
        compiler_params=_params(("arbitrary",)),
        name="attn_bias_table",
    )(rel_bias.astype(F32), bucket)


def _attn_kernel(sink_ref, q_ref, kp_ref, km_ref, kn_ref, vp_ref, vm_ref, vn_ref, tab_ref, o_ref, p_ref,
                 *, nb, nq, n_kv, q_per_kv):
    first = (pl.program_id(0) * nq) % nb
    col = lax.broadcasted_iota(jnp.int32, (1, 3 * BLOCK), 1)
    scale = HEAD_DIM ** -0.5
    n_rows = q_per_kv * BLOCK

    def band(prev_ref, main_ref, next_ref, j, ks):
        parts = []
        for b in (j - 1, j, j + 1):
            if b < 0:
                parts.append(prev_ref[:, ks])
            elif b >= nq:
                parts.append(next_ref[:, ks])
            else:
                parts.append(main_ref[b * BLOCK:(b + 1) * BLOCK, ks])
        return jnp.concatenate(parts, axis=0)

    for j in range(nq):
        n = first + j
        outside = ((col < BLOCK) & (n == 0)) | ((col >= 2 * BLOCK) & (n == nb - 1))
        qrows = slice(j * BLOCK, (j + 1) * BLOCK)
        for kh in range(n_kv):
            ks = slice(kh * HEAD_DIM, (kh + 1) * HEAD_DIM)
            kband = band(kp_ref, km_ref, kn_ref, j, ks)
            vband = band(vp_ref, vm_ref, vn_ref, j, ks)
            heads = list(range(kh * q_per_kv, (kh + 1) * q_per_kv))
            qh = jnp.concatenate([q_ref[qrows, h * HEAD_DIM:(h + 1) * HEAD_DIM] for h in heads], axis=0)
            s_all = lax.dot_general(qh, kband, (((1,), (1,)), ((), ())), preferred_element_type=F32)
            inv = []
            for c in range(0, n_rows, ATTN_ROWS):
                h = heads[c // BLOCK]
                s = s_all[c:c + ATTN_ROWS] * scale + tab_ref[h, c % BLOCK:c % BLOCK + ATTN_ROWS, :]
                s = jnp.where(outside, NEG_INF, s)
                sink = sink_ref[h]
                m = jnp.maximum(jnp.max(s, axis=-1, keepdims=True), sink)
                p = jnp.exp(s - m)
                denom = jnp.sum(p, axis=-1, keepdims=True) + jnp.exp(sink - m)
                p_ref[c:c + ATTN_ROWS, :] = p.astype(p_ref.dtype)
                inv.append(denom)
            o = jnp.dot(p_ref[...], vband, preferred_element_type=F32) / jnp.concatenate(inv, axis=0)
            for g, h in enumerate(heads):
                o_ref[qrows, h * HEAD_DIM:(h + 1) * HEAD_DIM] = o[g * BLOCK:(g + 1) * BLOCK].astype(o_ref.dtype)


def _attention(z, tab, sink, seq_len, attn_w, kv_w):
    rows = z.shape[0]
    nb = seq_len // BLOCK
    nq = math.gcd(nb, ATTN_BLOCKS_PER_STEP)
    nblk = rows // BLOCK
    n_kv = kv_w // HEAD_DIM
    q_per_kv = attn_w // kv_w
    assert attn_w % kv_w == 0
    kcol = attn_w // kv_w
    vcol = kcol + 1

    def prev(c):
        return lambda r: (jnp.maximum(r * nq - 1, 0), c)

    def main(c):
        return lambda r: (r, c)

    def nxt(c):
        return lambda r: (jnp.minimum((r + 1) * nq, nblk - 1), c)

    side_spec = lambda f, c: pl.BlockSpec((BLOCK, kv_w), f(c))
    main_spec = lambda c: pl.BlockSpec((nq * BLOCK, kv_w), main(c))
    return pl.pallas_call(
        functools.partial(_attn_kernel, nb=nb, nq=nq, n_kv=n_kv, q_per_kv=q_per_kv),
        grid=(nblk // nq,),
        in_specs=[pl.BlockSpec(memory_space=pltpu.SMEM),
                  pl.BlockSpec((nq * BLOCK, attn_w), lambda r: (r, 0)),
                  side_spec(prev, kcol), main_spec(kcol), side_spec(nxt, kcol),
                  side_spec(prev, vcol), main_spec(vcol), side_spec(nxt, vcol),
                  pl.BlockSpec(tab.shape, lambda r: (0, 0, 0))],
        out_specs=pl.BlockSpec((nq * BLOCK, attn_w), lambda r: (r, 0)),
        out_shape=jax.ShapeDtypeStruct((rows, attn_w), BF16),
        scratch_shapes=[pltpu.VMEM((q_per_kv * BLOCK, 3 * BLOCK), BF16)],
        compiler_params=_params(("parallel",)),
        name="windowed_gqa",
    )(sink.astype(F32), z, z, z, z, z, z, z, tab)


def _conv3_full(x, w, b, rows, n_rows):
    xm = jnp.where(rows == 0, 0.0, pltpu.roll(x, 1, 0))
    xp = jnp.where(rows == n_rows - 1, 0.0, pltpu.roll(x, n_rows - 1, 0))
    return xm * w[0:1] + x * w[1:2] + xp * w[2:3] + b


def _negate_index(x):
    n, _ = x.shape
    rb = min(n, PERM_ROWS)
    nbk = n // rb
    r = lax.broadcasted_iota(jnp.int32, (rb, rb), 0)
    c = lax.broadcasted_iota(jnp.int32, (rb, rb), 1)
    flip = (r + c == rb).astype(x.dtype)
    first = lax.broadcasted_iota(jnp.int32, (rb, 1), 0) == 0
    outs = []
    for bj in range(nbk):
        a = x[(nbk - 1 - bj) * rb:(nbk - bj) * rb]
        b0 = ((nbk - bj) % nbk) * rb
        out = jnp.dot(flip, a, preferred_element_type=F32).astype(x.dtype)
        outs.append(jnp.where(first, x[b0:b0 + 1], out))
    return jnp.concatenate(outs, axis=0)


def _hy_pre_kernel(v_ref, x1_ref, x0_ref, wv_ref, w1_ref, w0_ref, bv_ref, b1_ref, b0_ref,
                   ua_ref, ub_ref, xa_ref, xb_ref, e_ref, o_ref, side_ref):
    n_rows = v_ref.shape[1]
    half = n_rows // 2
    rows = lax.broadcasted_iota(jnp.int32, (n_rows, 1), 0)
    hv = _conv3_full(v_ref[0].astype(F32), wv_ref[...], bv_ref[...], rows, n_rows)
    hx1 = _conv3_full(x1_ref[0].astype(F32), w1_ref[...], b1_ref[...], rows, n_rows)
    u = (hv * hx1).astype(ua_ref.dtype)
    hx0 = _conv3_full(x0_ref[0].astype(F32), w0_ref[...], b0_ref[...], rows, n_rows).astype(xa_ref.dtype)
    ua_ref[0] = u[0:half]
    xa_ref[0] = hx0[0:half]
    ub = _negate_index(u[half:n_rows])
    ub_ref[0] = ub
    xb_ref[0] = _negate_index(hx0[half:n_rows])
    ua32 = u[0:half].astype(F32)
    ub32 = ub.astype(F32)
    first = rows[0:half] == 0
    e_ref[0] = jnp.where(first, ua32, ua32 + ub32).astype(e_ref.dtype)
    o_ref[0] = jnp.where(first, ua32, ua32 - ub32).astype(o_ref.dtype)
    u32 = u.astype(F32)
    nyq = jnp.sum(jnp.where(rows % 2 == 1, -u32, u32), axis=0, keepdims=True)
    side_ref[0] = jnp.concatenate([u32[half:half + 1], nyq, jnp.zeros((6, u32.shape[1]), F32)], axis=0)


def _hy_pre(z3, short_w, short_b, hy_off, hw):
    bsz, seq_len, _ = z3.shape
    half = seq_len // 2
    tc = LANES
    assert hy_off % tc == 0 and hw % tc == 0
    ob, nc = hy_off // tc, hw // tc
    zspec = lambda part: pl.BlockSpec((1, seq_len, tc), lambda b, c: (b, 0, ob + part * nc + c))
    wspec = lambda part: pl.BlockSpec((3, tc), lambda b, c: (0, part * nc + c))
    bspec = lambda part: pl.BlockSpec((1, tc), lambda b, c: (0, part * nc + c))
    ospec = pl.BlockSpec((1, half, tc), lambda b, c: (b, 0, c))
    sw = short_w.astype(F32)
    sb = short_b.reshape(1, -1).astype(F32)
    return pl.pallas_call(
        _hy_pre_kernel,
        grid=(bsz, nc),
        in_specs=[zspec(0), zspec(1), zspec(2), wspec(0), wspec(1), wspec(2), bspec(0), bspec(1), bspec(2)],
        out_specs=[ospec] * 6 + [pl.BlockSpec((1, 8, tc), lambda b, c: (b, 0, c))],
        out_shape=[jax.ShapeDtypeStruct((bsz, half, hw), BF16)] * 6 + [jax.ShapeDtypeStruct((bsz, 8, hw), F32)],
        compiler_params=_params(("parallel", "parallel")),
        name="hyena_short_conv",
    )(z3, z3, z3, sw, sw, sw, sb, sb, sb)


def _filter_mlp_kernel(zf_ref, w1_ref, b1_ref, f1_ref, w2_ref, b2_ref, f2_ref, w3f_ref, w3b_ref,
                       decf_ref, decb_ref, hs_ref, hd_ref):
    tl = zf_ref.shape[0]
    hp = lax.Precision.HIGHEST
    zf = zf_ref[...]
    h = jnp.sin(f1_ref[...] * (jnp.dot(zf, w1_ref[...], precision=hp, preferred_element_type=F32) + b1_ref[...]))
    h = jnp.sin(f2_ref[...] * (jnp.dot(h, w2_ref[...], precision=hp, preferred_element_type=F32) + b2_ref[...]))
    t = zf[:, 0:1]
    fwd = jnp.dot(h, w3f_ref[...], precision=hp, preferred_element_type=F32) * jnp.exp(-t * decf_ref[...])
    bwd = jnp.dot(h, w3b_ref[...], precision=hp, preferred_element_type=F32) * jnp.exp(-t * decb_ref[...])
    pos = pl.program_id(0) * tl + lax.broadcasted_iota(jnp.int32, (tl, 1), 0)
    bwd = jnp.where(pos == 0, 0.0, bwd)
    hs_ref[...] = (fwd + bwd).astype(hs_ref.dtype)
    hd_ref[...] = (fwd - bwd).astype(hd_ref.dtype)


def _pad2(a, rows, cols):
    return jnp.pad(a.astype(F32), ((0, rows - a.shape[0]), (0, cols - a.shape[1])))


def _filter_mlp(seq_len, w1, b1, f1, w2, b2, f2, w3, decay):
    hw = decay.shape[1]
    pos = jnp.arange(seq_len, dtype=F32)
    t = pos / (seq_len - 1)
    bands = jnp.linspace(1e-4, FILTER_BANDS - 1, FILTER_BANDS, dtype=F32)
    ang = (2.0 * math.pi / seq_len) * pos[:, None] * bands[None, :]
    zf = jnp.concatenate([t[:, None], jnp.cos(ang), -jnp.sin(ang)], axis=-1)
    emb, hid = w1.shape
    assert emb <= LANES and hid <= LANES
    zf = _pad2(zf, seq_len, LANES)
    row = lambda v: _pad2(v.reshape(1, -1), 1, LANES)
    w3p = _pad2(w3, LANES, 2 * hw)
    dec = decay.reshape(1, 2 * hw).astype(F32)
    tl = _pick(seq_len, 512, 8)
    tc = _pick(hw, 2048, LANES)
    nc = hw // tc
    small = lambda shape: pl.BlockSpec(shape, lambda i, c: (0, 0))
    return pl.pallas_call(
        _filter_mlp_kernel,
        grid=(seq_len // tl, nc),
        in_specs=[pl.BlockSpec((tl, LANES), lambda i, c: (i, 0)),
                  small((LANES, LANES)), small((1, LANES)), small((1, LANES)),
                  small((LANES, LANES)), small((1, LANES)), small((1, LANES)),
                  pl.BlockSpec((LANES, tc), lambda i, c: (0, c)),
                  pl.BlockSpec((LANES, tc), lambda i, c: (0, nc + c)),
                  pl.BlockSpec((1, tc), lambda i, c: (0, c)),
                  pl.BlockSpec((1, tc), lambda i, c: (0, nc + c))],
        out_specs=[pl.BlockSpec((tl, tc), lambda i, c: (i, c))] * 2,
        out_shape=[jax.ShapeDtypeStruct((seq_len, hw), BF16)] * 2,
        compiler_params=_params(("parallel", "parallel")),
        name="hyena_filter_mlp",
    )(zf, _pad2(w1, LANES, LANES), row(b1), row(f1), _pad2(w2, LANES, LANES), row(b2), row(f2),
      w3p, w3p, dec, dec)


def _trig_kernel(cb_ref, sb_ref, ck_ref, sk_ref, c_ref, s_ref, *, nyquist_row):
    tk, n_cols = cb_ref.shape
    cb, sb = cb_ref[...], sb_ref[...]
    ck, sk = ck_ref[0], sk_ref[0]
    sn = sk * cb + ck * sb
    if nyquist_row:
        k = pl.program_id(0) * tk + lax.broadcasted_iota(jnp.int32, (tk, 1), 0)
        n = lax.broadcasted_iota(jnp.int32, (1, n_cols), 1)
        sn = jnp.where(k == 0, jnp.where(n % 2 == 1, -1.0, 1.0), sn)
    c_ref[...] = (ck * cb - sk * sb).astype(c_ref.dtype)
    s_ref[...] = sn.astype(s_ref.dtype)


def _trig_matrices(n, n2, base_idx, tile_idx, name, n_cols=None, nyquist_row=False):
    tk = _pick(n, 256, HALO)
    n_i = n // tk
    rows, n = n, n_cols or n
    col = jnp.arange(n, dtype=jnp.int32)[None, :]

    def tables(idx):
        ang = (idx % n2).astype(F32) * (2.0 * math.pi / n2)
        return jnp.cos(ang), jnp.sin(ang)

    cb, sb = tables(base_idx(jnp.arange(tk, dtype=jnp.int32)[:, None], col))
    ck, sk = tables(tile_idx(jnp.arange(n_i, dtype=jnp.int32)[:, None] * tk, col))
    base = pl.BlockSpec((tk, n), lambda i: (0, 0))
    rowt = pl.BlockSpec((1, 1, n), lambda i: (i, 0, 0))
    ospec = pl.BlockSpec((tk, n), lambda i: (i, 0))
    return pl.pallas_call(
        functools.partial(_trig_kernel, nyquist_row=nyquist_row),
        grid=(n_i,),
        in_specs=[base, base, rowt, rowt],
        out_specs=[ospec] * 2,
        out_shape=[jax.ShapeDtypeStruct((rows, n), BF16)] * 2,
        compiler_params=_params(("parallel",)),
        name=name,
    )(cb, sb, ck.reshape(n_i, 1, n), sk.reshape(n_i, 1, n))


def _spectrum_mats(seq_len):
    half, n2 = seq_len // 2, 2 * seq_len
    even = _trig_matrices(half, n2, lambda d, c: 2 * d * c, lambda r, c: 2 * r * c, "dft_even_rows",
                          n_cols=seq_len, nyquist_row=True)
    odd = _trig_matrices(half, n2, lambda d, c: (2 * d + 1) * c, lambda r, c: 2 * r * c, "dft_odd_rows",
                         n_cols=seq_len)
    return even, odd


def _folded_dft_mats(seq_len):
    half, n2 = seq_len // 2, 2 * seq_len
    ce, se = _trig_matrices(half, n2, lambda d, c: 2 * d * c, lambda r, c: 2 * r * c, "dft_even_bins")
    co, so = _trig_matrices(half, n2, lambda d, c: (2 * d + 1) * c, lambda r, c: 2 * r * c, "dft_odd_bins")
    cot, sot = _trig_matrices(half, n2, lambda d, c: (2 * c + 1) * d, lambda r, c: (2 * c + 1) * r,
                              "dft_odd_bins_t")
    return ce, se, co, so, cot, sot


def _filter_spectrum_kernel(fc_ref, fs_ref, hs_ref, hd_ref, a_ref, d_ref, *, seq_len, even):
    tm = fc_ref.shape[0]
    hs = hs_ref[...]
    scale = 2.0 / (2 * seq_len)
    if even:
        row0 = (pl.program_id(0) * tm + lax.broadcasted_iota(jnp.int32, (tm, 1), 0)) == 0
        scale = jnp.where(row0, 1.0 / (2 * seq_len), scale)
    a_ref[...] = scale * jnp.dot(fc_ref[...], hs, preferred_element_type=F32)
    d_ref[...] = scale * jnp.dot(fs_ref[...], hd_ref[...], preferred_element_type=F32)
    if even:
        @pl.when(pl.program_id(0) == 0)
        def _():
            nyq = jnp.dot(fs_ref[0:HALO, :], hs, preferred_element_type=F32)[0:1]
            d_ref[0:1, :] = nyq * (1.0 / (2 * seq_len))


def _filter_spectrum(fc, fs, hs, hd, even):
    seq_len, hw = hs.shape
    half = fc.shape[0]
    tm = _pick(half, 512, HALO)
    tn = _pick(hw, 512, LANES)
    fspec = pl.BlockSpec((tm, seq_len), lambda i, j: (i, 0))
    hspec = pl.BlockSpec((seq_len, tn), lambda i, j: (0, j))
    ospec = pl.BlockSpec((tm, tn), lambda i, j: (i, j))
    return pl.pallas_call(
        functools.partial(_filter_spectrum_kernel, seq_len=seq_len, even=even),
        grid=(half // tm, hw // tn),
        in_specs=[fspec, fspec, hspec, hspec],
        out_specs=[ospec, ospec],
        out_shape=[jax.ShapeDtypeStruct((half, hw), F32)] * 2,
        compiler_params=_params(("parallel", "parallel")),
        name="hyena_filter_spectrum",
    )(fc, fs, hs, hd)


def _alt_sign(idx):
    return jnp.where(idx % 2 == 1, -1.0, 1.0)


def _dft_fwd_kernel(ce_ref, so_ref, co_ref, se_ref, e_ref, o_ref, side_ref, ae_ref, de_ref, ao_ref, do_ref,
                    w1e_ref, w2e_ref, w1o_ref, w2o_ref):
    tm = ce_ref.shape[0]
    e, o = e_ref[0], o_ref[0]
    pe_all = jnp.dot(ce_ref[...], e, preferred_element_type=F32)
    qo_all = jnp.dot(so_ref[...], e, preferred_element_type=F32)
    po_all = jnp.dot(co_ref[...], o, preferred_element_type=F32)
    qe_all = jnp.dot(se_ref[...], o, preferred_element_type=F32)
    mid = side_ref[0, 0:1, :]
    nyq = side_ref[0, 1:2, :]
    sub = _pick(tm, EPILOGUE_ROWS, 8)
    for r0 in range(0, tm, sub):
        rs = slice(r0, r0 + sub)
        m = pl.program_id(0) * tm + r0 + lax.broadcasted_iota(jnp.int32, (sub, 1), 0)
        corr = _alt_sign(m) * mid
        row0 = m == 0
        pe, qe, a, d = pe_all[rs] + corr, qe_all[rs], ae_ref[rs, :], de_ref[rs, :]
        qd = jnp.where(row0, nyq, qe) * d
        w1e_ref[0, rs, :] = (pe * a - jnp.where(row0, 0.0, qd)).astype(w1e_ref.dtype)
        w2e_ref[0, rs, :] = jnp.where(row0, qd, pe * d + qe * a).astype(w2e_ref.dtype)
        po, qo, a, d = po_all[rs], qo_all[rs] + corr, ao_ref[rs, :], do_ref[rs, :]
        w1o_ref[0, rs, :] = (po * a - qo * d).astype(w1o_ref.dtype)
        w2o_ref[0, rs, :] = (po * d + qo * a).astype(w2o_ref.dtype)


def _dft_fwd(mats, e, o, side, ae, de, ao, do):
    ce, se, co, so, _, _ = mats
    bsz, half, hw = e.shape
    tm = _pick(half, 512, HALO)
    tn = _pick(hw, 512, LANES)
    fspec = pl.BlockSpec((tm, half), lambda i, j, b: (i, 0))
    uspec = pl.BlockSpec((1, half, tn), lambda i, j, b: (b, 0, j))
    kspec = pl.BlockSpec((tm, tn), lambda i, j, b: (i, j))
    ospec = pl.BlockSpec((1, tm, tn), lambda i, j, b: (b, i, j))
    return pl.pallas_call(
        _dft_fwd_kernel,
        grid=(half // tm, hw // tn, bsz),
        in_specs=[fspec] * 4 + [uspec, uspec, pl.BlockSpec((1, 8, tn), lambda i, j, b: (b, 0, j))] + [kspec] * 4,
        out_specs=[ospec] * 4,
        out_shape=[jax.ShapeDtypeStruct((bsz, half, hw), BF16)] * 4,
        compiler_params=_params(("parallel", "parallel", "parallel")),
        name="hyena_dft_forward",
    )(ce, so, co, se, e, o, side, ae, de, ao, do)


def _dft_inv_kernel(ce_ref, sot_ref, cot_ref, se_ref, alt_ref, w1e_ref, w2e_ref, w1o_ref, w2o_ref,
                    ua_ref, ub_ref, xa_ref, xb_ref, skip_ref, oa_ref, ob_ref):
    tm = oa_ref.shape[1]
    half = w1e_ref.shape[1]
    w1e, w2o = w1e_ref[0], w2o_ref[0]
    s1_all = (jnp.dot(ce_ref[...], w1e, preferred_element_type=F32)
              + jnp.dot(sot_ref[...], w2o, preferred_element_type=F32))
    s2_all = (jnp.dot(cot_ref[...], w1o_ref[0], preferred_element_type=F32)
              + jnp.dot(se_ref[...], w2e_ref[0], preferred_element_type=F32))
    nyq = w2e_ref[0, 0:1, :].astype(F32)
    skip = skip_ref[...]
    sub = _pick(tm, EPILOGUE_ROWS, 8)
    for r0 in range(0, tm, sub):
        rs = slice(r0, r0 + sub)
        t = pl.program_id(0) * tm + r0 + lax.broadcasted_iota(jnp.int32, (sub, 1), 0)
        base = s1_all[rs] + _alt_sign(t) * nyq
        ya = base + s2_all[rs] + ua_ref[0, rs, :].astype(F32) * skip
        yb = base - s2_all[rs] + ub_ref[0, rs, :].astype(F32) * skip
        oa_ref[0, rs, :] = (ya * xa_ref[0, rs, :].astype(F32)).astype(oa_ref.dtype)
        ob_ref[0, rs, :] = (yb * xb_ref[0, rs, :].astype(F32)).astype(ob_ref.dtype)

    @pl.when(pl.program_id(0) == 0)
    def _():
        alt = alt_ref[...]
        mid = (jnp.dot(alt, w1e, preferred_element_type=F32) + jnp.dot(alt, w2o, preferred_element_type=F32))[0:1]
        y = mid + (1.0 if half % 2 == 0 else -1.0) * nyq + ub_ref[0, 0:1, :].astype(F32) * skip
        ob_ref[0, 0:1, :] = (y * xb_ref[0, 0:1, :].astype(F32)).astype(ob_ref.dtype)


def _dft_inv(mats, w1e, w2e, w1o, w2o, ua, ub, xa, xb, skip):
    ce, se, _, _, cot, sot = mats
    bsz, half, hw = ua.shape
    tm = _pick(half, 512, HALO)
    tn = _pick(hw, 512, LANES)
    alt = jnp.zeros((HALO, half), F32).at[0].set(_alt_sign(jnp.arange(half))).astype(BF16)
    fspec = pl.BlockSpec((tm, half), lambda i, j, b: (i, 0))
    wspec = pl.BlockSpec((1, half, tn), lambda i, j, b: (b, 0, j))
    tspec = pl.BlockSpec((1, tm, tn), lambda i, j, b: (b, i, j))
    return pl.pallas_call(
        _dft_inv_kernel,
        grid=(half // tm, hw // tn, bsz),
        in_specs=[fspec] * 4 + [pl.BlockSpec((HALO, half), lambda i, j, b: (0, 0))] + [wspec] * 4 + [tspec] * 4
                 + [pl.BlockSpec((1, tn), lambda i, j, b: (0, j))],
        out_specs=[tspec, tspec],
        out_shape=[jax.ShapeDtypeStruct((bsz, half, hw), BF16)] * 2,
        compiler_params=_params(("parallel", "parallel", "parallel")),
        name="hyena_dft_inverse",
    )(ce, sot, cot, se, alt, w1e, w2e, w1o, w2o, ua, ub, xa, xb, skip.reshape(1, hw).astype(F32))


def _unfold_kernel(oa_ref, ob_ref, o_ref):
    half = oa_ref.shape[1]
    o_ref[0, 0:half, :] = oa_ref[0]
    o_ref[0, half:2 * half, :] = _negate_index(ob_ref[0])


def _unfold(oa, ob):
    bsz, half, hw = oa.shape
    tc = _pick(hw, 512, LANES)
    ispec = pl.BlockSpec((1, half, tc), lambda b, c: (b, 0, c))
    return pl.pallas_call(
        _unfold_kernel,
        grid=(bsz, hw // tc),
        in_specs=[ispec, ispec],
        out_specs=pl.BlockSpec((1, 2 * half, tc), lambda b, c: (b, 0, c)),
        out_shape=jax.ShapeDtypeStruct((bsz, 2 * half, hw), oa.dtype),
        compiler_params=_params(("parallel", "parallel")),
        name="hyena_unfold",
    )(oa, ob)


def _ffn_up_kernel(x_hbm, xp_ref, xn_ref, g_ref, wg_ref, wv_ref, cw_ref, cb_ref, o_ref, h_ref, halo_ref,
                   x_buf, x_sem, *, seq_len, n_j):
    s = pl.program_id(0)
    tm = x_buf.shape[0]
    _norm_row_tile(x_hbm, x_buf, x_sem, g_ref, h_ref, s, n_j)

    @pl.when(s % n_j == 0)
    def _():
        g = g_ref[...]
        halo_ref[0:HALO] = _rms_scale(xp_ref[...], g).astype(halo_ref.dtype)
        halo_ref[HALO:2 * HALO] = _rms_scale(xn_ref[...], g).astype(halo_ref.dtype)

    h = h_ref[...]
    wg = wg_ref[...]
    gu = jnp.dot(h, wg, preferred_element_type=F32)
    val = jnp.dot(h, wv_ref[...], preferred_element_type=F32)
    hg = jnp.dot(halo_ref[...], wg, preferred_element_type=F32)
    row_start = (s // n_j) * tm
    above = jnp.where(row_start % seq_len == 0, 0.0, hg[HALO - 8:HALO])
    below = jnp.where((row_start + tm) % seq_len == 0, 0.0, hg[HALO:HALO + 8])
    cw = cw_ref[...]
    cb = cb_ref[...]
    sub = _pick(tm, EPILOGUE_ROWS, 8)
    ext = sub + 16
    for r0 in range(0, tm, sub):
        before = above if r0 == 0 else gu[r0 - 8:r0]
        after = below if r0 + sub == tm else gu[r0 + sub:r0 + sub + 8]
        g_ext = jnp.concatenate([before, gu[r0:r0 + sub], after], axis=0)
        g_m1 = pltpu.roll(g_ext, 1, 0)[8:8 + sub]
        g_p1 = pltpu.roll(g_ext, ext - 1, 0)[8:8 + sub]
        conv = g_m1 * cw[0:1] + g_ext[8:8 + sub] * cw[1:2] + g_p1 * cw[2:3] + cb
        gelu = 0.5 * conv * (1.0 + lax.erf(conv * math.sqrt(0.5)))
        o_ref[r0:r0 + sub, :] = (gelu * val[r0:r0 + sub]).astype(o_ref.dtype)


def _ffn_up(x, gain, w_up, conv_w, conv_b, seq_len):
    rows, d = x.shape
    d_ff = w_up.shape[1] // 2
    tm = _pick(seq_len, 1024, HALO)
    tn = _pick(d_ff, 256, LANES)
    n_i, n_j = rows // tm, d_ff // tn
    hb = tm // HALO
    n_halo = rows // HALO
    return pl.pallas_call(
        functools.partial(_ffn_up_kernel, seq_len=seq_len, n_j=n_j),
        grid=(n_i * n_j,),
        in_specs=[pl.BlockSpec(memory_space=pl.ANY),
                  pl.BlockSpec((HALO, d), lambda s: (jnp.maximum((s // n_j) * hb - 1, 0), 0)),
                  pl.BlockSpec((HALO, d), lambda s: (jnp.minimum((s // n_j + 1) * hb, n_halo - 1), 0)),
                  pl.BlockSpec((1, d), lambda s: (0, 0)),
                  pl.BlockSpec((d, tn), lambda s: (0, s % n_j)),
                  pl.BlockSpec((d, tn), lambda s: (0, n_j + s % n_j)),
                  pl.BlockSpec((3, tn), lambda s: (0, s % n_j)),
                  pl.BlockSpec((1, tn), lambda s: (0, s % n_j))],
        out_specs=pl.BlockSpec((tm, tn), lambda s: (s // n_j, s % n_j)),
        out_shape=jax.ShapeDtypeStruct((rows, d_ff), BF16),
        scratch_shapes=[pltpu.VMEM((tm, d), BF16), pltpu.VMEM((2 * HALO, d), BF16), pltpu.VMEM((tm, d), F32),
                        pltpu.SemaphoreType.DMA(())],
        compiler_params=_params(("arbitrary",)),
        name="convglu_up",
    )(x, x, x, gain.reshape(1, d).astype(F32), w_up, w_up, conv_w.astype(F32),
      conv_b.reshape(1, d_ff).astype(F32))


def _prep_layer(w):
    names = ("w_in", "w_attn_o", "w_hyena_o", "w_out", "w_up", "w_down", "w_ple_gate", "w_ple")
    return {k: w[k].astype(BF16) for k in names}


def _encoder_layer(x, p, tab, w, wp, seq_len):
    rows, d_model = x.shape
    bsz = rows // seq_len
    attn_w = w["w_attn_o"].shape[0]
    hw = w["w_hyena_o"].shape[0]
    in_cols = w["w_in"].shape[1]
    kv_w = (in_cols - attn_w - 3 * hw - 2 * d_model) // 2
    hy_off = attn_w + 2 * kv_w
    gate_off = hy_off + 3 * hw
    tm = _pick(seq_len, 1024, HALO)

    z = _fused_matmul([(x, wp["w_in"])], [], _ep_plain, BF16, in_cols, tm, _pick(in_cols, 1024, LANES),
                      "in_proj", norm_gain=w["g_mix"])
    attn = _attention(z, tab, w["attn_sink"], seq_len, attn_w, kv_w)

    ua, ub, xa, xb, e, o, side = _hy_pre(z.reshape(bsz, seq_len, in_cols), w["hy_short_w"], w["hy_short_b"],
                                         hy_off, hw)
    hs, hd = _filter_mlp(seq_len, w["hy_filt_w1"], w["hy_filt_b1"], w["hy_filt_f1"], w["hy_filt_w2"],
                         w["hy_filt_b2"], w["hy_filt_f2"], w["hy_filt_w3"], w["hy_decay"])
    (fce, fse), (fco, fso) = _spectrum_mats(seq_len)
    ae, de = _filter_spectrum(fce, fse, hs, hd, even=True)
    ao, do = _filter_spectrum(fco, fso, hs, hd, even=False)
    mats = _folded_dft_mats(seq_len)
    w1e, w2e, w1o, w2o = _dft_fwd(mats, e, o, side, ae, de, ao, do)
    oa, ob = _dft_inv(mats, w1e, w2e, w1o, w2o, ua, ub, xa, xb, w["hy_skip"])
    hy_out = _unfold(oa, ob).reshape(rows, hw)

    tn = _pick(math.gcd(d_model, gate_off), 512, LANES)
    merged = _fused_matmul([(attn, wp["w_attn_o"]), (hy_out, wp["w_hyena_o"])],
                           [(z, "tile", gate_off), (z, "tile", gate_off + d_model)],
                           _ep_branch_merge, BF16, d_model, tm, tn, "branch_merge")
    tn = _pick(d_model, 512, LANES)
    x = _fused_matmul([(merged, wp["w_out"])], [(x, "tile", 0)], _ep_residual, F32, d_model, tm, tn, "out_proj")

    act = _ffn_up(x, w["g_ffn"], wp["w_up"], w["ffn_conv_w"], w["ffn_conv_b"], seq_len)
    x = _fused_matmul([(act, wp["w_down"])], [(x, "tile", 0)], _ep_residual, F32, d_model,
                      _pick(seq_len, 512, HALO), _pick(d_model, 512, LANES), "ffn_down")

    x = _fused_matmul([(x, wp["w_ple_gate"]), (p.astype(BF16), wp["w_ple"])], [(x, "tile", 0)],
                      _ep_ple, F32, d_model, tm, tn, "ple_gate", norm_gain=w["g_ple"])
    return x


_LAYER_WEIGHT_NAMES = ("g_mix", "w_in", "attn_sink", "hy_short_w", "hy_short_b", "hy_filt_w1", "hy_filt_b1",
                       "hy_filt_f1", "hy_filt_w2", "hy_filt_b2", "hy_filt_f2", "hy_filt_w3", "hy_decay",
                       "hy_skip", "w_attn_o", "w_hyena_o", "w_out", "g_ffn", "w_up", "ffn_conv_w", "ffn_conv_b",
                       "w_down", "g_ple", "w_ple_gate", "w_ple")


def _trunk(x, p, tab, layers, preps, g_final):
    bsz, seq_len, d_model = x.shape
    xf = x.reshape(bsz * seq_len, d_model)
    for l, (w, wp) in enumerate(zip(layers, preps)):
        xf = _encoder_layer(xf, p[l].reshape(bsz * seq_len, -1), tab, w, wp, seq_len)
    return _rmsnorm(xf, g_final, x.dtype).reshape(bsz, seq_len, d_model)


def kernel(x_prompt, x_sample, p_prompt, p_sample, rel_bias, g_mix, w_in, attn_sink, hy_short_w, hy_short_b, hy_filt_w1, hy_filt_b1, hy_filt_f1, hy_filt_w2, hy_filt_b2, hy_filt_f2, hy_filt_w3, hy_decay, hy_skip, w_attn_o, w_hyena_o, w_out, g_ffn, w_up, ffn_conv_w, ffn_conv_b, w_down, g_ple, w_ple_gate, w_ple, g_final):
    stacked = (g_mix, w_in, attn_sink, hy_short_w, hy_short_b, hy_filt_w1, hy_filt_b1, hy_filt_f1, hy_filt_w2,
               hy_filt_b2, hy_filt_f2, hy_filt_w3, hy_decay, hy_skip, w_attn_o, w_hyena_o, w_out, g_ffn, w_up,
               ffn_conv_w, ffn_conv_b, w_down, g_ple, w_ple_gate, w_ple)
    depth = g_mix.shape[0]
    layers = [dict(zip(_LAYER_WEIGHT_NAMES, [a[l] for a in stacked])) for l in range(depth)]
    preps = [_prep_layer(w) for w in layers]
    tab = _bias_table(rel_bias)
    y_prompt = _trunk(x_prompt, p_prompt, tab, layers, preps, g_final)
    y_sample = _trunk(x_sample, p_sample, tab, layers, preps, g_final)
    return (y_prompt, y_sample)
```

```python
import functools
import math

import jax
import jax.numpy as jnp
from jax import lax
from jax.experimental import pallas as pl
from jax.experimental.pallas import tpu as pltpu

F32 = jnp.float32
BF16 = jnp.bfloat16

EPS = 1e-6
NEG_INF = -1e30
HEAD_DIM = 128
WINDOW = 128
BLOCK = 128
N_BUCKETS = 32
MAX_DISTANCE = 128
FILTER_BANDS = 16

V7X_VMEM_LIMIT_BYTES = 56 * 1024 * 1024
LANES = 128
HALO = 16
EPILOGUE_ROWS = 32
NORM_ROWS = 32
ATTN_ROWS = 32
ATTN_BLOCKS_PER_STEP = 4
PERM_ROWS = 256


def _pick(n, pref, align):
    t = min(pref, n)
    t -= t % align
    while t >= align:
        if n % t == 0:
            return t
        t -= align
    return n


def _params(sem):
    return pltpu.CompilerParams(dimension_semantics=sem, vmem_limit_bytes=V7X_VMEM_LIMIT_BYTES)


def _rms_scale(x, g):
    ms = jnp.mean(x * x, axis=-1, keepdims=True)
    return x * lax.rsqrt(ms + EPS) * g


def _rms_rows_to(x_ref, g_ref, h_ref):
    rows = x_ref.shape[0]
    chunk = _pick(rows, NORM_ROWS, 8)
    g = g_ref[...]

    def body(c, carry):
        r = pl.ds(pl.multiple_of(c * chunk, chunk), chunk)
        h_ref[r, :] = _rms_scale(x_ref[r, :].astype(F32), g).astype(h_ref.dtype)
        return carry

    n_chunks = rows // chunk
    lax.fori_loop(0, n_chunks, body, 0, unroll=math.gcd(n_chunks, 8))


def _row_tile_copy(x_hbm, x_buf, sem, i):
    tm = x_buf.shape[0]
    return pltpu.make_async_copy(x_hbm.at[pl.ds(i * tm, tm), :], x_buf, sem)


def _norm_row_tile(x_hbm, x_buf, sem, g_ref, h_ref, s, n_j):
    i = s // n_j
    n_i = x_hbm.shape[0] // x_buf.shape[0]

    @pl.when(s == 0)
    def _():
        _row_tile_copy(x_hbm, x_buf, sem, 0).start()

    @pl.when(s % n_j == 0)
    def _():
        _row_tile_copy(x_hbm, x_buf, sem, i).wait()
        _rms_rows_to(x_buf, g_ref, h_ref)

        @pl.when(i + 1 < n_i)
        def _():
            _row_tile_copy(x_hbm, x_buf, sem, i + 1).start()


def _rmsnorm_kernel(x_ref, g_ref, o_ref):
    o_ref[...] = _rms_scale(x_ref[...].astype(F32), g_ref[...]).astype(o_ref.dtype)


def _rmsnorm(x, g, out_dtype):
    rows, d = x.shape
    tm = _pick(rows, 512, 8)
    return pl.pallas_call(
        _rmsnorm_kernel,
        grid=(rows // tm,),
        in_specs=[pl.BlockSpec((tm, d), lambda i: (i, 0)), pl.BlockSpec((1, d), lambda i: (0, 0))],
        out_specs=pl.BlockSpec((tm, d), lambda i: (i, 0)),
        out_shape=jax.ShapeDtypeStruct((rows, d), out_dtype),
        compiler_params=_params(("parallel",)),
        name="rmsnorm",
    )(x, g.reshape(1, d).astype(F32))


def _fused_matmul_kernel(*refs, n_pairs, n_extras, epilogue, norm, n_j):
    s = pl.program_id(0)
    pos = 0
    a_refs, b_refs = [], []
    for p in range(n_pairs):
        a_refs.append(refs[pos])
        pos += 1
        if norm and p == 0:
            g_ref = refs[pos]
            pos += 1
        b_refs.append(refs[pos])
        pos += 1
    extra_refs = refs[pos:pos + n_extras]
    o_ref = refs[pos + n_extras]
    if norm:
        h_ref, x_buf, x_sem = refs[pos + n_extras + 1:pos + n_extras + 4]
        _norm_row_tile(a_refs[0], x_buf, x_sem, g_ref, h_ref, s, n_j)
        a_refs[0] = h_ref

    dots = [jnp.dot(a_refs[p][...], b_refs[p][...], preferred_element_type=F32) for p in range(n_pairs)]
    tm = o_ref.shape[0]
    sub = _pick(tm, EPILOGUE_ROWS, 8)
    for r0 in range(0, tm, sub):
        rs = slice(r0, r0 + sub)
        extras = [r[rs, :] if r.shape[0] == tm else r[...] for r in extra_refs]
        o_ref[rs, :] = epilogue([d[rs] for d in dots], extras).astype(o_ref.dtype)


def _fused_matmul(pairs, extras, epilogue, out_dtype, n_cols, tm, tn, name, norm_gain=None):
    rows = pairs[0][0].shape[0]
    assert rows % tm == 0 and n_cols % tn == 0
    n_i, n_j = rows // tm, n_cols // tn
    norm = norm_gain is not None
    args, in_specs, scratch = [], [], []
    for p, (a, b) in enumerate(pairs):
        k = a.shape[1]
        assert b.shape[0] == k
        if norm and p == 0:
            args += [a, norm_gain.reshape(1, k).astype(F32), b]
            in_specs += [pl.BlockSpec(memory_space=pl.ANY), pl.BlockSpec((1, k), lambda s: (0, 0))]
            scratch += [pltpu.VMEM((tm, k), BF16), pltpu.VMEM((tm, k), F32), pltpu.SemaphoreType.DMA(())]
        else:
            args += [a, b]
            in_specs.append(pl.BlockSpec((tm, k), lambda s: (s // n_j, 0)))
        in_specs.append(pl.BlockSpec((k, tn), lambda s: (0, s % n_j)))
    for arr, kind, off in extras:
        assert off % tn == 0
        ob = off // tn
        args.append(arr)
        if kind == "tile":
            in_specs.append(pl.BlockSpec((tm, tn), lambda s, ob=ob: (s // n_j, s % n_j + ob)))
        else:
            in_specs.append(pl.BlockSpec((1, tn), lambda s, ob=ob: (0, s % n_j + ob)))
    return pl.pallas_call(
        functools.partial(_fused_matmul_kernel, n_pairs=len(pairs), n_extras=len(extras), epilogue=epilogue,
                          norm=norm, n_j=n_j),
        grid=(n_i * n_j,),
        in_specs=in_specs,
        out_specs=pl.BlockSpec((tm, tn), lambda s: (s // n_j, s % n_j)),
        out_shape=jax.ShapeDtypeStruct((rows, n_cols), out_dtype),
        scratch_shapes=scratch,
        compiler_params=_params(("arbitrary",)),
        name=name,
    )(*args)


def _ep_plain(dots, extras):
    return dots[0]


def _sigmoid(x):
    return 0.5 * (1.0 + jnp.tanh(0.5 * x))


def _ep_branch_merge(dots, extras):
    ga, gh = extras
    return _sigmoid(ga.astype(F32)) * dots[0] + _sigmoid(gh.astype(F32)) * dots[1]


def _ep_residual(dots, extras):
    return extras[0] + dots[0]


def _ep_ple(dots, extras):
    return extras[0] + _sigmoid(dots[0]) * dots[1]


def _bias_table_kernel(rb_ref, bkt_ref, o_ref):
    h = pl.program_id(0)
    bkt = bkt_ref[...]
    q = lax.broadcasted_iota(jnp.int32, bkt.shape, 0)
    s = lax.broadcasted_iota(jnp.int32, bkt.shape, 1)
    rel = s - BLOCK - q
    acc = jnp.zeros(bkt.shape, F32)
    for b in range(N_BUCKETS):
        acc = jnp.where(bkt == b, rb_ref[b, h], acc)
    o_ref[0] = jnp.where(jnp.abs(rel) <= WINDOW, acc, NEG_INF)


def _bias_table(rel_bias):
    n_heads = rel_bias.shape[1]
    half = N_BUCKETS // 2
    max_exact = half // 2
    qi = jnp.arange(BLOCK)[:, None]
    sj = jnp.arange(3 * BLOCK)[None, :]
    rel = sj - BLOCK - qi
    ret = jnp.where(rel > 0, half, 0)
    n = jnp.abs(rel)
    nf = jnp.maximum(n, 1).astype(F32)
    large = max_exact + (jnp.log(nf / max_exact) / math.log(MAX_DISTANCE / max_exact)
                         * (half - max_exact)).astype(jnp.int32)
    large = jnp.minimum(large, half - 1)
    bucket = (ret + jnp.where(n < max_exact, n, large)).astype(jnp.int32)
    return pl.pallas_call(
        _bias_table_kernel,
        grid=(n_heads,),
        in_specs=[pl.BlockSpec(memory_space=pltpu.SMEM),
                  pl.BlockSpec((BLOCK, 3 * BLOCK), lambda h: (0, 0))],
        out_specs=pl.BlockSpec((1, BLOCK, 3 * BLOCK), lambda h: (h, 0, 0)),
        out_shape=jax.ShapeDtypeStruct((n_heads, BLOCK, 3 * BLOCK), F32),
        compiler_params=_params(("arbitrary",)),
        name="attn_bias_table",
    )(rel_bias.astype(F32), bucket)


def _attn_kernel(sink_ref, q_ref, kp_ref, km_ref, kn_ref, vp_ref, vm_ref, vn_ref, tab_ref, o_ref, p_ref,
                 *, nb, nq, n_kv, q_per_kv):
    first = (pl.program_id(0) * nq) % nb
    col = lax.broadcasted_iota(jnp.int32, (1, 3 * BLOCK), 1)
    scale = HEAD_DIM ** -0.5
    n_rows = q_per_kv * BLOCK

    def band(prev_ref, main_ref, next_ref, j, ks):
        parts = []
        for b in (j - 1, j, j + 1):
            if b < 0:
                parts.append(prev_ref[:, ks])
            elif b >= nq:
                parts.append(next_ref[:, ks])
            else:
                parts.append(main_ref[b * BLOCK:(b + 1) * BLOCK, ks])
        return jnp.concatenate(parts, axis=0)

    for j in range(nq):
        n = first + j
        outside = ((col < BLOCK) & (n == 0)) | ((col >= 2 * BLOCK) & (n == nb - 1))
        qrows = slice(j * BLOCK, (j + 1) * BLOCK)
        for kh in range(n_kv):
            ks = slice(kh * HEAD_DIM, (kh + 1) * HEAD_DIM)
            kband = band(kp_ref, km_ref, kn_ref, j, ks)
            vband = band(vp_ref, vm_ref, vn_ref, j, ks)
            heads = list(range(kh * q_per_kv, (kh + 1) * q_per_kv))
            qh = jnp.concatenate([q_ref[qrows, h * HEAD_DIM:(h + 1) * HEAD_DIM] for h in heads], axis=0)
            s_all = lax.dot_general(qh, kband, (((1,), (1,)), ((), ())), preferred_element_type=F32)
            inv = []
            for c in range(0, n_rows, ATTN_ROWS):
                h = heads[c // BLOCK]
                s = s_all[c:c + ATTN_ROWS] * scale + tab_ref[h, c % BLOCK:c % BLOCK + ATTN_ROWS, :]
                s = jnp.where(outside, NEG_INF, s)
                sink = sink_ref[h]
                m = jnp.maximum(jnp.max(s, axis=-1, keepdims=True), sink)
                p = jnp.exp(s - m)
                denom = jnp.sum(p, axis=-1, keepdims=True) + jnp.exp(sink - m)
                p_ref[c:c + ATTN_ROWS, :] = p.astype(p_ref.dtype)
                inv.append(denom)
            o = jnp.dot(p_ref[...], vband, preferred_element_type=F32) / jnp.concatenate(inv, axis=0)
            for g, h in enumerate(heads):
                o_ref[qrows, h * HEAD_DIM:(h + 1) * HEAD_DIM] = o[g * BLOCK:(g + 1) * BLOCK].astype(o_ref.dtype)


def _attention(z, tab, sink, seq_len, attn_w, kv_w):
    rows = z.shape[0]
    nb = seq_len // BLOCK
    nq = math.gcd(nb, ATTN_BLOCKS_PER_STEP)
    nblk = rows // BLOCK
    n_kv = kv_w // HEAD_DIM
    q_per_kv = attn_w // kv_w
    assert attn_w % kv_w == 0
    kcol = attn_w // kv_w
    vcol = kcol + 1

    def prev(c):
        return lambda r: (jnp.maximum(r * nq - 1, 0), c)

    def main(c):
        return lambda r: (r, c)

    def nxt(c):
        return lambda r: (jnp.minimum((r + 1) * nq, nblk - 1), c)

    side_spec = lambda f, c: pl.BlockSpec((BLOCK, kv_w), f(c))
    main_spec = lambda c: pl.BlockSpec((nq * BLOCK, kv_w), main(c))
    return pl.pallas_call(
        functools.partial(_attn_kernel, nb=nb, nq=nq, n_kv=n_kv, q_per_kv=q_per_kv),
        grid=(nblk // nq,),
        in_specs=[pl.BlockSpec(memory_space=pltpu.SMEM),
                  pl.BlockSpec((nq * BLOCK, attn_w), lambda r: (r, 0)),
                  side_spec(prev, kcol), main_spec(kcol), side_spec(nxt, kcol),
                  side_spec(prev, vcol), main_spec(vcol), side_spec(nxt, vcol),
                  pl.BlockSpec(tab.shape, lambda r: (0, 0, 0))],
        out_specs=pl.BlockSpec((nq * BLOCK, attn_w), lambda r: (r, 0)),
        out_shape=jax.ShapeDtypeStruct((rows, attn_w), BF16),
        scratch_shapes=[pltpu.VMEM((q_per_kv * BLOCK, 3 * BLOCK), BF16)],
        compiler_params=_params(("parallel",)),
        name="windowed_gqa",
    )(sink.astype(F32), z, z, z, z, z, z, z, tab)


def _conv3_full(x, w, b, rows, n_rows):
    xm = jnp.where(rows == 0, 0.0, pltpu.roll(x, 1, 0))
    xp = jnp.where(rows == n_rows - 1, 0.0, pltpu.roll(x, n_rows - 1, 0))
    return xm * w[0:1] + x * w[1:2] + xp * w[2:3] + b


def _negate_index(x):
    n, _ = x.shape
    rb = min(n, PERM_ROWS)
    nbk = n // rb
    r = lax.broadcasted_iota(jnp.int32, (rb, rb), 0)
    c = lax.broadcasted_iota(jnp.int32, (rb, rb), 1)
    flip = (r + c == rb).astype(x.dtype)
    first = lax.broadcasted_iota(jnp.int32, (rb, 1), 0) == 0
    outs = []
    for bj in range(nbk):
        a = x[(nbk - 1 - bj) * rb:(nbk - bj) * rb]
        b0 = ((nbk - bj) % nbk) * rb
        out = jnp.dot(flip, a, preferred_element_type=F32).astype(x.dtype)
        outs.append(jnp.where(first, x[b0:b0 + 1], out))
    return jnp.concatenate(outs, axis=0)


def _hy_pre_kernel(v_ref, x1_ref, x0_ref, wv_ref, w1_ref, w0_ref, bv_ref, b1_ref, b0_ref,
                   ua_ref, ub_ref, xa_ref, xb_ref, e_ref, o_ref, side_ref):
    n_rows = v_ref.shape[1]
    half = n_rows // 2
    rows = lax.broadcasted_iota(jnp.int32, (n_rows, 1), 0)
    hv = _conv3_full(v_ref[0].astype(F32), wv_ref[...], bv_ref[...], rows, n_rows)
    hx1 = _conv3_full(x1_ref[0].astype(F32), w1_ref[...], b1_ref[...], rows, n_rows)
    u = (hv * hx1).astype(ua_ref.dtype)
    hx0 = _conv3_full(x0_ref[0].astype(F32), w0_ref[...], b0_ref[...], rows, n_rows).astype(xa_ref.dtype)
    ua_ref[0] = u[0:half]
    xa_ref[0] = hx0[0:half]
    ub = _negate_index(u[half:n_rows])
    ub_ref[0] = ub
    xb_ref[0] = _negate_index(hx0[half:n_rows])
    ua32 = u[0:half].astype(F32)
    ub32 = ub.astype(F32)
    first = rows[0:half] == 0
    e_ref[0] = jnp.where(first, ua32, ua32 + ub32).astype(e_ref.dtype)
    o_ref[0] = jnp.where(first, ua32, ua32 - ub32).astype(o_ref.dtype)
    u32 = u.astype(F32)
    nyq = jnp.sum(jnp.where(rows % 2 == 1, -u32, u32), axis=0, keepdims=True)
    side_ref[0] = jnp.concatenate([u32[half:half + 1], nyq, jnp.zeros((6, u32.shape[1]), F32)], axis=0)


def _hy_pre(z3, short_w, short_b, hy_off, hw):
    bsz, seq_len, _ = z3.shape
    half = seq_len // 2
    tc = LANES
    assert hy_off % tc == 0 and hw % tc == 0
    ob, nc = hy_off // tc, hw // tc
    zspec = lambda part: pl.BlockSpec((1, seq_len, tc), lambda b, c: (b, 0, ob + part * nc + c))
    wspec = lambda part: pl.BlockSpec((3, tc), lambda b, c: (0, part * nc + c))
    bspec = lambda part: pl.BlockSpec((1, tc), lambda b, c: (0, part * nc + c))
    ospec = pl.BlockSpec((1, half, tc), lambda b, c: (b, 0, c))
    sw = short_w.astype(F32)
    sb = short_b.reshape(1, -1).astype(F32)
    return pl.pallas_call(
        _hy_pre_kernel,
        grid=(bsz, nc),
        in_specs=[zspec(0), zspec(1), zspec(2), wspec(0), wspec(1), wspec(2), bspec(0), bspec(1), bspec(2)],
        out_specs=[ospec] * 6 + [pl.BlockSpec((1, 8, tc), lambda b, c: (b, 0, c))],
        out_shape=[jax.ShapeDtypeStruct((bsz, half, hw), BF16)] * 6 + [jax.ShapeDtypeStruct((bsz, 8, hw), F32)],
        compiler_params=_params(("parallel", "parallel")),
        name="hyena_short_conv",
    )(z3, z3, z3, sw, sw, sw, sb, sb, sb)


def _filter_mlp_kernel(zf_ref, w1_ref, b1_ref, f1_ref, w2_ref, b2_ref, f2_ref, w3f_ref, w3b_ref,
                       decf_ref, decb_ref, hs_ref, hd_ref):
    tl = zf_ref.shape[0]
    hp = lax.Precision.HIGHEST
    zf = zf_ref[...]
    h = jnp.sin(f1_ref[...] * (jnp.dot(zf, w1_ref[...], precision=hp, preferred_element_type=F32) + b1_ref[...]))
    h = jnp.sin(f2_ref[...] * (jnp.dot(h, w2_ref[...], precision=hp, preferred_element_type=F32) + b2_ref[...]))
    t = zf[:, 0:1]
    fwd = jnp.dot(h, w3f_ref[...], precision=hp, preferred_element_type=F32) * jnp.exp(-t * decf_ref[...])
    bwd = jnp.dot(h, w3b_ref[...], precision=hp, preferred_element_type=F32) * jnp.exp(-t * decb_ref[...])
    pos = pl.program_id(0) * tl + lax.broadcasted_iota(jnp.int32, (tl, 1), 0)
    bwd = jnp.where(pos == 0, 0.0, bwd)
    hs_ref[...] = (fwd + bwd).astype(hs_ref.dtype)
    hd_ref[...] = (fwd - bwd).astype(hd_ref.dtype)


def _pad2(a, rows, cols):
    return jnp.pad(a.astype(F32), ((0, rows - a.shape[0]), (0, cols - a.shape[1])))


def _filter_mlp(seq_len, w1, b1, f1, w2, b2, f2, w3, decay):
    hw = decay.shape[1]
    pos = jnp.arange(seq_len, dtype=F32)
    t = pos / (seq_len - 1)
    bands = jnp.linspace(1e-4, FILTER_BANDS - 1, FILTER_BANDS, dtype=F32)
    ang = (2.0 * math.pi / seq_len) * pos[:, None] * bands[None, :]
    zf = jnp.concatenate([t[:, None], jnp.cos(ang), -jnp.sin(ang)], axis=-1)
    emb, hid = w1.shape
    assert emb <= LANES and hid <= LANES
    zf = _pad2(zf, seq_len, LANES)
    row = lambda v: _pad2(v.reshape(1, -1), 1, LANES)
    w3p = _pad2(w3, LANES, 2 * hw)
    dec = decay.reshape(1, 2 * hw).astype(F32)
    tl = _pick(seq_len, 512, 8)
    tc = _pick(hw, 2048, LANES)
    nc = hw // tc
    small = lambda shape: pl.BlockSpec(shape, lambda i, c: (0, 0))
    return pl.pallas_call(
        _filter_mlp_kernel,
        grid=(seq_len // tl, nc),
        in_specs=[pl.BlockSpec((tl, LANES), lambda i, c: (i, 0)),
                  small((LANES, LANES)), small((1, LANES)), small((1, LANES)),
                  small((LANES, LANES)), small((1, LANES)), small((1, LANES)),
                  pl.BlockSpec((LANES, tc), lambda i, c: (0, c)),
                  pl.BlockSpec((LANES, tc), lambda i, c: (0, nc + c)),
                  pl.BlockSpec((1, tc), lambda i, c: (0, c)),
                  pl.BlockSpec((1, tc), lambda i, c: (0, nc + c))],
        out_specs=[pl.BlockSpec((tl, tc), lambda i, c: (i, c))] * 2,
        out_shape=[jax.ShapeDtypeStruct((seq_len, hw), BF16)] * 2,
        compiler_params=_params(("parallel", "parallel")),
        name="hyena_filter_mlp",
    )(zf, _pad2(w1, LANES, LANES), row(b1), row(f1), _pad2(w2, LANES, LANES), row(b2), row(f2),
      w3p, w3p, dec, dec)


def _trig_kernel(cb_ref, sb_ref, ck_ref, sk_ref, c_ref, s_ref, *, nyquist_row):
    tk, n_cols = cb_ref.shape
    cb, sb = cb_ref[...], sb_ref[...]
    ck, sk = ck_ref[0], sk_ref[0]
    sn = sk * cb + ck * sb
    if nyquist_row:
        k = pl.program_id(0) * tk + lax.broadcasted_iota(jnp.int32, (tk, 1), 0)
        n = lax.broadcasted_iota(jnp.int32, (1, n_cols), 1)
        sn = jnp.where(k == 0, jnp.where(n % 2 == 1, -1.0, 1.0), sn)
    c_ref[...] = (ck * cb - sk * sb).astype(c_ref.dtype)
    s_ref[...] = sn.astype(s_ref.dtype)


def _trig_matrices(n, n2, base_idx, tile_idx, name, n_cols=None, nyquist_row=False):
    tk = _pick(n, 256, HALO)
    n_i = n // tk
    rows, n = n, n_cols or n
    col = jnp.arange(n, dtype=jnp.int32)[None, :]

    def tables(idx):
        ang = (idx % n2).astype(F32) * (2.0 * math.pi / n2)
        return jnp.cos(ang), jnp.sin(ang)

    cb, sb = tables(base_idx(jnp.arange(tk, dtype=jnp.int32)[:, None], col))
    ck, sk = tables(tile_idx(jnp.arange(n_i, dtype=jnp.int32)[:, None] * tk, col))
    base = pl.BlockSpec((tk, n), lambda i: (0, 0))
    rowt = pl.BlockSpec((1, 1, n), lambda i: (i, 0, 0))
    ospec = pl.BlockSpec((tk, n), lambda i: (i, 0))
    return pl.pallas_call(
        functools.partial(_trig_kernel, nyquist_row=nyquist_row),
        grid=(n_i,),
        in_specs=[base, base, rowt, rowt],
        out_specs=[ospec] * 2,
        out_shape=[jax.ShapeDtypeStruct((rows, n), BF16)] * 2,
        compiler_params=_params(("parallel",)),
        name=name,
    )(cb, sb, ck.reshape(n_i, 1, n), sk.reshape(n_i, 1, n))


def _spectrum_mats(seq_len):
    half, n2 = seq_len // 2, 2 * seq_len
    even = _trig_matrices(half, n2, lambda d, c: 2 * d * c, lambda r, c: 2 * r * c, "dft_even_rows",
                          n_cols=seq_len, nyquist_row=True)
    odd = _trig_matrices(half, n2, lambda d, c: (2 * d + 1) * c, lambda r, c: 2 * r * c, "dft_odd_rows",
                         n_cols=seq_len)
    return even, odd


def _folded_dft_mats(seq_len):
    half, n2 = seq_len // 2, 2 * seq_len
    ce, se = _trig_matrices(half, n2, lambda d, c: 2 * d * c, lambda r, c: 2 * r * c, "dft_even_bins")
    co, so = _trig_matrices(half, n2, lambda d, c: (2 * d + 1) * c, lambda r, c: 2 * r * c, "dft_odd_bins")
    cot, sot = _trig_matrices(half, n2, lambda d, c: (2 * c + 1) * d, lambda r, c: (2 * c + 1) * r,
                              "dft_odd_bins_t")
    return ce, se, co, so, cot, sot


def _filter_spectrum_kernel(fc_ref, fs_ref, hs_ref, hd_ref, a_ref, d_ref, *, seq_len, even):
    tm = fc_ref.shape[0]
    hs = hs_ref[...]
    scale = 2.0 / (2 * seq_len)
    if even:
        row0 = (pl.program_id(0) * tm + lax.broadcasted_iota(jnp.int32, (tm, 1), 0)) == 0
        scale = jnp.where(row0, 1.0 / (2 * seq_len), scale)
    a_ref[...] = scale * jnp.dot(fc_ref[...], hs, preferred_element_type=F32)
    d_ref[...] = scale * jnp.dot(fs_ref[...], hd_ref[...], preferred_element_type=F32)
    if even:
        @pl.when(pl.program_id(0) == 0)
        def _():
            nyq = jnp.dot(fs_ref[0:HALO, :], hs, preferred_element_type=F32)[0:1]
            d_ref[0:1, :] = nyq * (1.0 / (2 * seq_len))


def _filter_spectrum(fc, fs, hs, hd, even):
    seq_len, hw = hs.shape
    half = fc.shape[0]
    tm = _pick(half, 512, HALO)
    tn = _pick(hw, 512, LANES)
    fspec = pl.BlockSpec((tm, seq_len), lambda i, j: (i, 0))
    hspec = pl.BlockSpec((seq_len, tn), lambda i, j: (0, j))
    ospec = pl.BlockSpec((tm, tn), lambda i, j: (i, j))
    return pl.pallas_call(
        functools.partial(_filter_spectrum_kernel, seq_len=seq_len, even=even),
        grid=(half // tm, hw // tn),
        in_specs=[fspec, fspec, hspec, hspec],
        out_specs=[ospec, ospec],
        out_shape=[jax.ShapeDtypeStruct((half, hw), F32)] * 2,
        compiler_params=_params(("parallel", "parallel")),
        name="hyena_filter_spectrum",
    )(fc, fs, hs, hd)


def _alt_sign(idx):
    return jnp.where(idx % 2 == 1, -1.0, 1.0)


def _dft_fwd_kernel(ce_ref, so_ref, co_ref, se_ref, e_ref, o_ref, side_ref, ae_ref, de_ref, ao_ref, do_ref,
                    w1e_ref, w2e_ref, w1o_ref, w2o_ref):
    tm = ce_ref.shape[0]
    e, o = e_ref[0], o_ref[0]
    pe_all = jnp.dot(ce_ref[...], e, preferred_element_type=F32)
    qo_all = jnp.dot(so_ref[...], e, preferred_element_type=F32)
    po_all = jnp.dot(co_ref[...], o, preferred_element_type=F32)
    qe_all = jnp.dot(se_ref[...], o, preferred_element_type=F32)
    mid = side_ref[0, 0:1, :]
    nyq = side_ref[0, 1:2, :]
    sub = _pick(tm, EPILOGUE_ROWS, 8)
    for r0 in range(0, tm, sub):
        rs = slice(r0, r0 + sub)
        m = pl.program_id(0) * tm + r0 + lax.broadcasted_iota(jnp.int32, (sub, 1), 0)
        corr = _alt_sign(m) * mid
        row0 = m == 0
        pe, qe, a, d = pe_all[rs] + corr, qe_all[rs], ae_ref[rs, :], de_ref[rs, :]
        qd = jnp.where(row0, nyq, qe) * d
        w1e_ref[0, rs, :] = (pe * a - jnp.where(row0, 0.0, qd)).astype(w1e_ref.dtype)
        w2e_ref[0, rs, :] = jnp.where(row0, qd, pe * d + qe * a).astype(w2e_ref.dtype)
        po, qo, a, d = po_all[rs], qo_all[rs] + corr, ao_ref[rs, :], do_ref[rs, :]
        w1o_ref[0, rs, :] = (po * a - qo * d).astype(w1o_ref.dtype)
        w2o_ref[0, rs, :] = (po * d + qo * a).astype(w2o_ref.dtype)


def _dft_fwd(mats, e, o, side, ae, de, ao, do):
    ce, se, co, so, _, _ = mats
    bsz, half, hw = e.shape
    tm = _pick(half, 512, HALO)
    tn = _pick(hw, 512, LANES)
    fspec = pl.BlockSpec((tm, half), lambda i, j, b: (i, 0))
    uspec = pl.BlockSpec((1, half, tn), lambda i, j, b: (b, 0, j))
    kspec = pl.BlockSpec((tm, tn), lambda i, j, b: (i, j))
    ospec = pl.BlockSpec((1, tm, tn), lambda i, j, b: (b, i, j))
    return pl.pallas_call(
        _dft_fwd_kernel,
        grid=(half // tm, hw // tn, bsz),
        in_specs=[fspec] * 4 + [uspec, uspec, pl.BlockSpec((1, 8, tn), lambda i, j, b: (b, 0, j))] + [kspec] * 4,
        out_specs=[ospec] * 4,
        out_shape=[jax.ShapeDtypeStruct((bsz, half, hw), BF16)] * 4,
        compiler_params=_params(("parallel", "parallel", "parallel")),
        name="hyena_dft_forward",
    )(ce, so, co, se, e, o, side, ae, de, ao, do)


def _dft_inv_kernel(ce_ref, sot_ref, cot_ref, se_ref, alt_ref, w1e_ref, w2e_ref, w1o_ref, w2o_ref,
                    ua_ref, ub_ref, xa_ref, xb_ref, skip_ref, oa_ref, ob_ref):
    tm = oa_ref.shape[1]
    half = w1e_ref.shape[1]
    w1e, w2o = w1e_ref[0], w2o_ref[0]
    s1_all = (jnp.dot(ce_ref[...], w1e, preferred_element_type=F32)
              + jnp.dot(sot_ref[...], w2o, preferred_element_type=F32))
    s2_all = (jnp.dot(cot_ref[...], w1o_ref[0], preferred_element_type=F32)
              + jnp.dot(se_ref[...], w2e_ref[0], preferred_element_type=F32))
    nyq = w2e_ref[0, 0:1, :].astype(F32)
    skip = skip_ref[...]
    sub = _pick(tm, EPILOGUE_ROWS, 8)
    for r0 in range(0, tm, sub):
        rs = slice(r0, r0 + sub)
        t = pl.program_id(0) * tm + r0 + lax.broadcasted_iota(jnp.int32, (sub, 1), 0)
        base = s1_all[rs] + _alt_sign(t) * nyq
        ya = base + s2_all[rs] + ua_ref[0, rs, :].astype(F32) * skip
        yb = base - s2_all[rs] + ub_ref[0, rs, :].astype(F32) * skip
        oa_ref[0, rs, :] = (ya * xa_ref[0, rs, :].astype(F32)).astype(oa_ref.dtype)
        ob_ref[0, rs, :] = (yb * xb_ref[0, rs, :].astype(F32)).astype(ob_ref.dtype)

    @pl.when(pl.program_id(0) == 0)
    def _():
        alt = alt_ref[...]
        mid = (jnp.dot(alt, w1e, preferred_element_type=F32) + jnp.dot(alt, w2o, preferred_element_type=F32))[0:1]
        y = mid + (1.0 if half % 2 == 0 else -1.0) * nyq + ub_ref[0, 0:1, :].astype(F32) * skip
        ob_ref[0, 0:1, :] = (y * xb_ref[0, 0:1, :].astype(F32)).astype(ob_ref.dtype)


def _dft_inv(mats, w1e, w2e, w1o, w2o, ua, ub, xa, xb, skip):
    ce, se, _, _, cot, sot = mats
    bsz, half, hw = ua.shape
    tm = _pick(half, 512, HALO)
    tn = _pick(hw, 512, LANES)
    alt = jnp.zeros((HALO, half), F32).at[0].set(_alt_sign(jnp.arange(half))).astype(BF16)
    fspec = pl.BlockSpec((tm, half), lambda i, j, b: (i, 0))
    wspec = pl.BlockSpec((1, half, tn), lambda i, j, b: (b, 0, j))
    tspec = pl.BlockSpec((1, tm, tn), lambda i, j, b: (b, i, j))
    return pl.pallas_call(
        _dft_inv_kernel,
        grid=(half // tm, hw // tn, bsz),
        in_specs=[fspec] * 4 + [pl.BlockSpec((HALO, half), lambda i, j, b: (0, 0))] + [wspec] * 4 + [tspec] * 4
                 + [pl.BlockSpec((1, tn), lambda i, j, b: (0, j))],
        out_specs=[tspec, tspec],
        out_shape=[jax.ShapeDtypeStruct((bsz, half, hw), BF16)] * 2,
        compiler_params=_params(("parallel", "parallel", "parallel")),
        name="hyena_dft_inverse",
    )(ce, sot, cot, se, alt, w1e, w2e, w1o, w2o, ua, ub, xa, xb, skip.reshape(1, hw).astype(F32))


def _unfold_kernel(oa_ref, ob_ref, o_ref):
    half = oa_ref.shape[1]
    o_ref[0, 0:half, :] = oa_ref[0]
    o_ref[0, half:2 * half, :] = _negate_index(ob_ref[0])


def _unfold(oa, ob):
    bsz, half, hw = oa.shape
    tc = _pick(hw, 512, LANES)
    ispec = pl.BlockSpec((1, half, tc), lambda b, c: (b, 0, c))
    return pl.pallas_call(
        _unfold_kernel,
        grid=(bsz, hw // tc),
        in_specs=[ispec, ispec],
        out_specs=pl.BlockSpec((1, 2 * half, tc), lambda b, c: (b, 0, c)),
        out_shape=jax.ShapeDtypeStruct((bsz, 2 * half, hw), oa.dtype),
        compiler_params=_params(("parallel", "parallel")),
        name="hyena_unfold",
    )(oa, ob)


def _ffn_up_kernel(x_hbm, xp_ref, xn_ref, g_ref, wg_ref, wv_ref, cw_ref, cb_ref, o_ref, h_ref, halo_ref,
                   x_buf, x_sem, *, seq_len, n_j):
    s = pl.program_id(0)
    tm = x_buf.shape[0]
    _norm_row_tile(x_hbm, x_buf, x_sem, g_ref, h_ref, s, n_j)

    @pl.when(s % n_j == 0)
    def _():
        g = g_ref[...]
        halo_ref[0:HALO] = _rms_scale(xp_ref[...], g).astype(halo_ref.dtype)
        halo_ref[HALO:2 * HALO] = _rms_scale(xn_ref[...], g).astype(halo_ref.dtype)

    h = h_ref[...]
    wg = wg_ref[...]
    gu = jnp.dot(h, wg, preferred_element_type=F32)
    val = jnp.dot(h, wv_ref[...], preferred_element_type=F32)
    hg = jnp.dot(halo_ref[...], wg, preferred_element_type=F32)
    row_start = (s // n_j) * tm
    above = jnp.where(row_start % seq_len == 0, 0.0, hg[HALO - 8:HALO])
    below = jnp.where((row_start + tm) % seq_len == 0, 0.0, hg[HALO:HALO + 8])
    cw = cw_ref[...]
    cb = cb_ref[...]
    sub = _pick(tm, EPILOGUE_ROWS, 8)
    ext = sub + 16
    for r0 in range(0, tm, sub):
        before = above if r0 == 0 else gu[r0 - 8:r0]
        after = below if r0 + sub == tm else gu[r0 + sub:r0 + sub + 8]
        g_ext = jnp.concatenate([before, gu[r0:r0 + sub], after], axis=0)
        g_m1 = pltpu.roll(g_ext, 1, 0)[8:8 + sub]
        g_p1 = pltpu.roll(g_ext, ext - 1, 0)[8:8 + sub]
        conv = g_m1 * cw[0:1] + g_ext[8:8 + sub] * cw[1:2] + g_p1 * cw[2:3] + cb
        gelu = 0.5 * conv * (1.0 + lax.erf(conv * math.sqrt(0.5)))
        o_ref[r0:r0 + sub, :] = (gelu * val[r0:r0 + sub]).astype(o_ref.dtype)


def _ffn_up(x, gain, w_up, conv_w, conv_b, seq_len):
    rows, d = x.shape
    d_ff = w_up.shape[1] // 2
    tm = _pick(seq_len, 1024, HALO)
    tn = _pick(d_ff, 256, LANES)
    n_i, n_j = rows // tm, d_ff // tn
    hb = tm // HALO
    n_halo = rows // HALO
    return pl.pallas_call(
        functools.partial(_ffn_up_kernel, seq_len=seq_len, n_j=n_j),
        grid=(n_i * n_j,),
        in_specs=[pl.BlockSpec(memory_space=pl.ANY),
                  pl.BlockSpec((HALO, d), lambda s: (jnp.maximum((s // n_j) * hb - 1, 0), 0)),
                  pl.BlockSpec((HALO, d), lambda s: (jnp.minimum((s // n_j + 1) * hb, n_halo - 1), 0)),
                  pl.BlockSpec((1, d), lambda s: (0, 0)),
                  pl.BlockSpec((d, tn), lambda s: (0, s % n_j)),
                  pl.BlockSpec((d, tn), lambda s: (0, n_j + s % n_j)),
                  pl.BlockSpec((3, tn), lambda s: (0, s % n_j)),
                  pl.BlockSpec((1, tn), lambda s: (0, s % n_j))],
        out_specs=pl.BlockSpec((tm, tn), lambda s: (s // n_j, s % n_j)),
        out_shape=jax.ShapeDtypeStruct((rows, d_ff), BF16),
        scratch_shapes=[pltpu.VMEM((tm, d), BF16), pltpu.VMEM((2 * HALO, d), BF16), pltpu.VMEM((tm, d), F32),
                        pltpu.SemaphoreType.DMA(())],
        compiler_params=_params(("arbitrary",)),
        name="convglu_up",
    )(x, x, x, gain.reshape(1, d).astype(F32), w_up, w_up, conv_w.astype(F32),
      conv_b.reshape(1, d_ff).astype(F32))


def _prep_layer(w):
    names = ("w_in", "w_attn_o", "w_hyena_o", "w_out", "w_up", "w_down", "w_ple_gate", "w_ple")
    return {k: w[k].astype(BF16) for k in names}


def _encoder_layer(x, p, tab, w, wp, seq_len):
    rows, d_model = x.shape
    bsz = rows // seq_len
    attn_w = w["w_attn_o"].shape[0]
    hw = w["w_hyena_o"].shape[0]
    in_cols = w["w_in"].shape[1]
    kv_w = (in_cols - attn_w - 3 * hw - 2 * d_model) // 2
    hy_off = attn_w + 2 * kv_w
    gate_off = hy_off + 3 * hw
    tm = _pick(seq_len, 1024, HALO)

    z = _fused_matmul([(x, wp["w_in"])], [], _ep_plain, BF16, in_cols, tm, _pick(in_cols, 1024, LANES),
                      "in_proj", norm_gain=w["g_mix"])
    attn = _attention(z, tab, w["attn_sink"], seq_len, attn_w, kv_w)

    ua, ub, xa, xb, e, o, side = _hy_pre(z.reshape(bsz, seq_len, in_cols), w["hy_short_w"], w["hy_short_b"],
                                         hy_off, hw)
    hs, hd = _filter_mlp(seq_len, w["hy_filt_w1"], w["hy_filt_b1"], w["hy_filt_f1"], w["hy_filt_w2"],
                         w["hy_filt_b2"], w["hy_filt_f2"], w["hy_filt_w3"], w["hy_decay"])
    (fce, fse), (fco, fso) = _spectrum_mats(seq_len)
    ae, de = _filter_spectrum(fce, fse, hs, hd, even=True)
    ao, do = _filter_spectrum(fco, fso, hs, hd, even=False)
    mats = _folded_dft_mats(seq_len)
    w1e, w2e, w1o, w2o = _dft_fwd(mats, e, o, side, ae, de, ao, do)
    oa, ob = _dft_inv(mats, w1e, w2e, w1o, w2o, ua, ub, xa, xb, w["hy_skip"])
    hy_out = _unfold(oa, ob).reshape(rows, hw)

    tn = _pick(math.gcd(d_model, gate_off), 1024, LANES)
    merged = _fused_matmul([(attn, wp["w_attn_o"]), (hy_out, wp["w_hyena_o"])],
                           [(z, "tile", gate_off), (z, "tile", gate_off + d_model)],
                           _ep_branch_merge, BF16, d_model, tm, tn, "branch_merge")
    x = _fused_matmul([(merged, wp["w_out"])], [(x, "tile", 0)], _ep_residual, F32, d_model, tm,
                      _pick(d_model, 1024, LANES), "out_proj")
    tn = _pick(d_model, 512, LANES)

    act = _ffn_up(x, w["g_ffn"], wp["w_up"], w["ffn_conv_w"], w["ffn_conv_b"], seq_len)
    x = _fused_matmul([(act, wp["w_down"])], [(x, "tile", 0)], _ep_residual, F32, d_model,
                      _pick(seq_len, 512, HALO), _pick(d_model, 512, LANES), "ffn_down")

    x = _fused_matmul([(x, wp["w_ple_gate"]), (p.astype(BF16), wp["w_ple"])], [(x, "tile", 0)],
                      _ep_ple, F32, d_model, tm, tn, "ple_gate", norm_gain=w["g_ple"])
    return x


_LAYER_WEIGHT_NAMES = ("g_mix", "w_in", "attn_sink", "hy_short_w", "hy_short_b", "hy_filt_w1", "hy_filt_b1",
                       "hy_filt_f1", "hy_filt_w2", "hy_filt_b2", "hy_filt_f2", "hy_filt_w3", "hy_decay",
                       "hy_skip", "w_attn_o", "w_hyena_o", "w_out", "g_ffn", "w_up", "ffn_conv_w", "ffn_conv_b",
                       "w_down", "g_ple", "w_ple_gate", "w_ple")


def _trunk(x, p, tab, layers, preps, g_final):
    bsz, seq_len, d_model = x.shape
    xf = x.reshape(bsz * seq_len, d_model)
    for l, (w, wp) in enumerate(zip(layers, preps)):
        xf = _encoder_layer(xf, p[l].reshape(bsz * seq_len, -1), tab, w, wp, seq_len)
    return _rmsnorm(xf, g_final, x.dtype).reshape(bsz, seq_len, d_model)


def kernel(x_prompt, x_sample, p_prompt, p_sample, rel_bias, g_mix, w_in, attn_sink, hy_short_w, hy_short_b, hy_filt_w1, hy_filt_b1, hy_filt_f1, hy_filt_w2, hy_filt_b2, hy_filt_f2, hy_filt_w3, hy_decay, hy_skip, w_attn_o, w_hyena_o, w_out, g_ffn, w_up, ffn_conv_w, ffn_conv_b, w_down, g_ple, w_ple_gate, w_ple, g_final):
    stacked = (g_mix, w_in, attn_sink, hy_short_w, hy_short_b, hy_filt_w1, hy_filt_b1, hy_filt_f1, hy_filt_w2,
               hy_filt_b2, hy_filt_f2, hy_filt_w3, hy_decay, hy_skip, w_attn_o, w_hyena_o, w_out, g_ffn, w_up,
               ffn_conv_w, ffn_conv_b, w_down, g_ple, w_ple_gate, w_ple)
    depth = g_mix.shape[0]
    layers = [dict(zip(_LAYER_WEIGHT_NAMES, [a[l] for a in stacked])) for l in range(depth)]
    preps = [_prep_layer(w) for w in layers]
    tab = _bias_table(rel_bias)
    y_prompt = _trunk(x_prompt, p_prompt, tab, layers, preps, g_final)
    y_sample = _trunk(x_sample, p_sample, tab, layers, preps, g_final)
    return (y_prompt, y_sample)
```

```python
import functools
import math

import jax
import jax.numpy as jnp
from jax import lax
from jax.experimental import pallas as pl
from jax.experimental.pallas import tpu as pltpu

F32 = jnp.float32
BF16 = jnp.bfloat16

EPS = 1e-6
NEG_INF = -1e30
HEAD_DIM = 128
WINDOW = 128
BLOCK = 128
N_BUCKETS = 32
MAX_DISTANCE = 128
FILTER_BANDS = 16

V7X_VMEM_LIMIT_BYTES = 56 * 1024 * 1024
LANES = 128
HALO = 16
EPILOGUE_ROWS = 32
NORM_ROWS = 16
ATTN_ROWS = 32
ATTN_BLOCKS_PER_STEP = 4
PERM_ROWS = 256


def _pick(n, pref, align):
    t = min(pref, n)
    t -= t % align
    while t >= align:
        if n % t == 0:
            return t
        t -= align
    return n


def _params(sem):
    return pltpu.CompilerParams(dimension_semantics=sem, vmem_limit_bytes=V7X_VMEM_LIMIT_BYTES)


def _rms_scale(x, g):
    ms = jnp.mean(x * x, axis=-1, keepdims=True)
    return x * lax.rsqrt(ms + EPS) * g


def _rms_rows_to(x_ref, g_ref, h_ref):
    rows = x_ref.shape[0]
    chunk = _pick(rows, NORM_ROWS, 8)
    g = g_ref[...]

    def body(c, carry):
        r = pl.ds(pl.multiple_of(c * chunk, chunk), chunk)
        h_ref[r, :] = _rms_scale(x_ref[r, :].astype(F32), g).astype(h_ref.dtype)
        return carry

    n_chunks = rows // chunk
    lax.fori_loop(0, n_chunks, body, 0, unroll=math.gcd(n_chunks, 8))


def _row_tile_copy(x_hbm, x_buf, sem, i):
    tm = x_buf.shape[0]
    return pltpu.make_async_copy(x_hbm.at[pl.ds(i * tm, tm), :], x_buf, sem)


def _norm_row_tile(x_hbm, x_buf, sem, g_ref, h_ref, s, n_j):
    i = s // n_j
    n_i = x_hbm.shape[0] // x_buf.shape[0]

    @pl.when(s == 0)
    def _():
        _row_tile_copy(x_hbm, x_buf, sem, 0).start()

    @pl.when(s % n_j == 0)
    def _():
        _row_tile_copy(x_hbm, x_buf, sem, i).wait()
        _rms_rows_to(x_buf, g_ref, h_ref)

        @pl.when(i + 1 < n_i)
        def _():
            _row_tile_copy(x_hbm, x_buf, sem, i + 1).start()


def _rmsnorm_kernel(x_ref, g_ref, o_ref):
    o_ref[...] = _rms_scale(x_ref[...].astype(F32), g_ref[...]).astype(o_ref.dtype)


def _rmsnorm(x, g, out_dtype):
    rows, d = x.shape
    tm = _pick(rows, 512, 8)
    return pl.pallas_call(
        _rmsnorm_kernel,
        grid=(rows // tm,),
        in_specs=[pl.BlockSpec((tm, d), lambda i: (i, 0)), pl.BlockSpec((1, d), lambda i: (0, 0))],
        out_specs=pl.BlockSpec((tm, d), lambda i: (i, 0)),
        out_shape=jax.ShapeDtypeStruct((rows, d), out_dtype),
        compiler_params=_params(("parallel",)),
        name="rmsnorm",
    )(x, g.reshape(1, d).astype(F32))


def _fused_matmul_kernel(*refs, n_pairs, n_extras, epilogue, norm, n_j):
    s = pl.program_id(0)
    pos = 0
    a_refs, b_refs = [], []
    for p in range(n_pairs):
        a_refs.append(refs[pos])
        pos += 1
        if norm and p == 0:
            g_ref = refs[pos]
            pos += 1
        b_refs.append(refs[pos])
        pos += 1
    extra_refs = refs[pos:pos + n_extras]
    o_ref = refs[pos + n_extras]
    if norm:
        h_ref, x_buf, x_sem = refs[pos + n_extras + 1:pos + n_extras + 4]
        _norm_row_tile(a_refs[0], x_buf, x_sem, g_ref, h_ref, s, n_j)
        a_refs[0] = h_ref

    dots = [jnp.dot(a_refs[p][...], b_refs[p][...], preferred_element_type=F32) for p in range(n_pairs)]
    tm = o_ref.shape[0]
    sub = _pick(tm, EPILOGUE_ROWS, 8)
    for r0 in range(0, tm, sub):
        rs = slice(r0, r0 + sub)
        extras = [r[rs, :] if r.shape[0] == tm else r[...] for r in extra_refs]
        o_ref[rs, :] = epilogue([d[rs] for d in dots], extras).astype(o_ref.dtype)


def _fused_matmul(pairs, extras, epilogue, out_dtype, n_cols, tm, tn, name, norm_gain=None):
    rows = pairs[0][0].shape[0]
    assert rows % tm == 0 and n_cols % tn == 0
    n_i, n_j = rows // tm, n_cols // tn
    norm = norm_gain is not None
    args, in_specs, scratch = [], [], []
    for p, (a, b) in enumerate(pairs):
        k = a.shape[1]
        assert b.shape[0] == k
        if norm and p == 0:
            args += [a, norm_gain.reshape(1, k).astype(F32), b]
            in_specs += [pl.BlockSpec(memory_space=pl.ANY), pl.BlockSpec((1, k), lambda s: (0, 0))]
            scratch += [pltpu.VMEM((tm, k), BF16), pltpu.VMEM((tm, k), F32), pltpu.SemaphoreType.DMA(())]
        else:
            args += [a, b]
            in_specs.append(pl.BlockSpec((tm, k), lambda s: (s // n_j, 0)))
        in_specs.append(pl.BlockSpec((k, tn), lambda s: (0, s % n_j)))
    for arr, kind, off in extras:
        assert off % tn == 0
        ob = off // tn
        args.append(arr)
        if kind == "tile":
            in_specs.append(pl.BlockSpec((tm, tn), lambda s, ob=ob: (s // n_j, s % n_j + ob)))
        else:
            in_specs.append(pl.BlockSpec((1, tn), lambda s, ob=ob: (0, s % n_j + ob)))
    return pl.pallas_call(
        functools.partial(_fused_matmul_kernel, n_pairs=len(pairs), n_extras=len(extras), epilogue=epilogue,
                          norm=norm, n_j=n_j),
        grid=(n_i * n_j,),
        in_specs=in_specs,
        out_specs=pl.BlockSpec((tm, tn), lambda s: (s // n_j, s % n_j)),
        out_shape=jax.ShapeDtypeStruct((rows, n_cols), out_dtype),
        scratch_shapes=scratch,
        compiler_params=_params(("arbitrary",)),
        name=name,
    )(*args)


def _ep_plain(dots, extras):
    return dots[0]


def _sigmoid(x):
    return 0.5 * (1.0 + jnp.tanh(0.5 * x))


def _ep_branch_merge(dots, extras):
    ga, gh = extras
    return _sigmoid(ga.astype(F32)) * dots[0] + _sigmoid(gh.astype(F32)) * dots[1]


def _ep_residual(dots, extras):
    return extras[0] + dots[0]


def _ep_ple(dots, extras):
    return extras[0] + _sigmoid(dots[0]) * dots[1]


def _bias_table_kernel(rb_ref, bkt_ref, o_ref):
    h = pl.program_id(0)
    bkt = bkt_ref[...]
    q = lax.broadcasted_iota(jnp.int32, bkt.shape, 0)
    s = lax.broadcasted_iota(jnp.int32, bkt.shape, 1)
    rel = s - BLOCK - q
    acc = jnp.zeros(bkt.shape, F32)
    for b in range(N_BUCKETS):
        acc = jnp.where(bkt == b, rb_ref[b, h], acc)
    o_ref[0] = jnp.where(jnp.abs(rel) <= WINDOW, acc, NEG_INF)


def _bias_table(rel_bias):
    n_heads = rel_bias.shape[1]
    half = N_BUCKETS // 2
    max_exact = half // 2
    qi = jnp.arange(BLOCK)[:, None]
    sj = jnp.arange(3 * BLOCK)[None, :]
    rel = sj - BLOCK - qi
    ret = jnp.where(rel > 0, half, 0)
    n = jnp.abs(rel)
    nf = jnp.maximum(n, 1).astype(F32)
    large = max_exact + (jnp.log(nf / max_exact) / math.log(MAX_DISTANCE / max_exact)
                         * (half - max_exact)).astype(jnp.int32)
    large = jnp.minimum(large, half - 1)
    bucket = (ret + jnp.where(n < max_exact, n, large)).astype(jnp.int32)
    return pl.pallas_call(
        _bias_table_kernel,
        grid=(n_heads,),
        in_specs=[pl.BlockSpec(memory_space=pltpu.SMEM),
                  pl.BlockSpec((BLOCK, 3 * BLOCK), lambda h: (0, 0))],
        out_specs=pl.BlockSpec((1, BLOCK, 3 * BLOCK), lambda h: (h, 0, 0)),
        out_shape=jax.ShapeDtypeStruct((n_heads, BLOCK, 3 * BLOCK), F32),
        compiler_params=_params(("arbitrary",)),
        name="attn_bias_table",
    )(rel_bias.astype(F32), bucket)


def _attn_kernel(sink_ref, q_ref, kp_ref, km_ref, kn_ref, vp_ref, vm_ref, vn_ref, tab_ref, o_ref, p_ref,
                 *, nb, nq, n_kv, q_per_kv):
    first = (pl.program_id(0) * nq) % nb
    col = lax.broadcasted_iota(jnp.int32, (1, 3 * BLOCK), 1)
    scale = HEAD_DIM ** -0.5
    n_rows = q_per_kv * BLOCK

    def band(prev_ref, main_ref, next_ref, j, ks):
        parts = []
        for b in (j - 1, j, j + 1):
            if b < 0:
                parts.append(prev_ref[:, ks])
            elif b >= nq:
                parts.append(next_ref[:, ks])
            else:
                parts.append(main_ref[b * BLOCK:(b + 1) * BLOCK, ks])
        return jnp.concatenate(parts, axis=0)

    for j in range(nq):
        n = first + j
        outside = ((col < BLOCK) & (n == 0)) | ((col >= 2 * BLOCK) & (n == nb - 1))
        qrows = slice(j * BLOCK, (j + 1) * BLOCK)
        for kh in range(n_kv):
            ks = slice(kh * HEAD_DIM, (kh + 1) * HEAD_DIM)
            kband = band(kp_ref, km_ref, kn_ref, j, ks)
            vband = band(vp_ref, vm_ref, vn_ref, j, ks)
            heads = list(range(kh * q_per_kv, (kh + 1) * q_per_kv))
            qh = jnp.concatenate([q_ref[qrows, h * HEAD_DIM:(h + 1) * HEAD_DIM] for h in heads], axis=0)
            s_all = lax.dot_general(qh, kband, (((1,), (1,)), ((), ())), preferred_element_type=F32)
            inv = []
            for c in range(0, n_rows, ATTN_ROWS):
                h = heads[c // BLOCK]
                s = s_all[c:c + ATTN_ROWS] * scale + tab_ref[h, c % BLOCK:c % BLOCK + ATTN_ROWS, :]
                s = jnp.where(outside, NEG_INF, s)
                sink = sink_ref[h]
                m = jnp.maximum(jnp.max(s, axis=-1, keepdims=True), sink)
                p = jnp.exp(s - m)
                denom = jnp.sum(p, axis=-1, keepdims=True) + jnp.exp(sink - m)
                p_ref[c:c + ATTN_ROWS, :] = p.astype(p_ref.dtype)
                inv.append(denom)
            o = jnp.dot(p_ref[...], vband, preferred_element_type=F32) / jnp.concatenate(inv, axis=0)
            for g, h in enumerate(heads):
                o_ref[qrows, h * HEAD_DIM:(h + 1) * HEAD_DIM] = o[g * BLOCK:(g + 1) * BLOCK].astype(o_ref.dtype)


def _attention(z, tab, sink, seq_len, attn_w, kv_w):
    rows = z.shape[0]
    nb = seq_len // BLOCK
    nq = math.gcd(nb, ATTN_BLOCKS_PER_STEP)
    nblk = rows // BLOCK
    n_kv = kv_w // HEAD_DIM
    q_per_kv = attn_w // kv_w
    assert attn_w % kv_w == 0
    kcol = attn_w // kv_w
    vcol = kcol + 1

    def prev(c):
        return lambda r: (jnp.maximum(r * nq - 1, 0), c)

    def main(c):
        return lambda r: (r, c)

    def nxt(c):
        return lambda r: (jnp.minimum((r + 1) * nq, nblk - 1), c)

    side_spec = lambda f, c: pl.BlockSpec((BLOCK, kv_w), f(c))
    main_spec = lambda c: pl.BlockSpec((nq * BLOCK, kv_w), main(c))
    return pl.pallas_call(
        functools.partial(_attn_kernel, nb=nb, nq=nq, n_kv=n_kv, q_per_kv=q_per_kv),
        grid=(nblk // nq,),
        in_specs=[pl.BlockSpec(memory_space=pltpu.SMEM),
                  pl.BlockSpec((nq * BLOCK, attn_w), lambda r: (r, 0)),
                  side_spec(prev, kcol), main_spec(kcol), side_spec(nxt, kcol),
                  side_spec(prev, vcol), main_spec(vcol), side_spec(nxt, vcol),
                  pl.BlockSpec(tab.shape, lambda r: (0, 0, 0))],
        out_specs=pl.BlockSpec((nq * BLOCK, attn_w), lambda r: (r, 0)),
        out_shape=jax.ShapeDtypeStruct((rows, attn_w), BF16),
        scratch_shapes=[pltpu.VMEM((q_per_kv * BLOCK, 3 * BLOCK), BF16)],
        compiler_params=_params(("parallel",)),
        name="windowed_gqa",
    )(sink.astype(F32), z, z, z, z, z, z, z, tab)


def _conv3_full(x, w, b, rows, n_rows):
    xm = jnp.where(rows == 0, 0.0, pltpu.roll(x, 1, 0))
    xp = jnp.where(rows == n_rows - 1, 0.0, pltpu.roll(x, n_rows - 1, 0))
    return xm * w[0:1] + x * w[1:2] + xp * w[2:3] + b


def _negate_index(x):
    n, _ = x.shape
    rb = min(n, PERM_ROWS)
    nbk = n // rb
    r = lax.broadcasted_iota(jnp.int32, (rb, rb), 0)
    c = lax.broadcasted_iota(jnp.int32, (rb, rb), 1)
    flip = (r + c == rb).astype(x.dtype)
    first = lax.broadcasted_iota(jnp.int32, (rb, 1), 0) == 0
    outs = []
    for bj in range(nbk):
        a = x[(nbk - 1 - bj) * rb:(nbk - bj) * rb]
        b0 = ((nbk - bj) % nbk) * rb
        out = jnp.dot(flip, a, preferred_element_type=F32).astype(x.dtype)
        outs.append(jnp.where(first, x[b0:b0 + 1], out))
    return jnp.concatenate(outs, axis=0)


def _hy_pre_kernel(v_ref, x1_ref, x0_ref, wv_ref, w1_ref, w0_ref, bv_ref, b1_ref, b0_ref,
                   ua_ref, ub_ref, xa_ref, xb_ref, e_ref, o_ref, side_ref):
    n_rows = v_ref.shape[1]
    half = n_rows // 2
    rows = lax.broadcasted_iota(jnp.int32, (n_rows, 1), 0)
    hv = _conv3_full(v_ref[0].astype(F32), wv_ref[...], bv_ref[...], rows, n_rows)
    hx1 = _conv3_full(x1_ref[0].astype(F32), w1_ref[...], b1_ref[...], rows, n_rows)
    u = (hv * hx1).astype(ua_ref.dtype)
    hx0 = _conv3_full(x0_ref[0].astype(F32), w0_ref[...], b0_ref[...], rows, n_rows).astype(xa_ref.dtype)
    ua_ref[0] = u[0:half]
    xa_ref[0] = hx0[0:half]
    ub = _negate_index(u[half:n_rows])
    ub_ref[0] = ub
    xb_ref[0] = _negate_index(hx0[half:n_rows])
    ua32 = u[0:half].astype(F32)
    ub32 = ub.astype(F32)
    first = rows[0:half] == 0
    e_ref[0] = jnp.where(first, ua32, ua32 + ub32).astype(e_ref.dtype)
    o_ref[0] = jnp.where(first, ua32, ua32 - ub32).astype(o_ref.dtype)
    u32 = u.astype(F32)
    nyq = jnp.sum(jnp.where(rows % 2 == 1, -u32, u32), axis=0, keepdims=True)
    side_ref[0] = jnp.concatenate([u32[half:half + 1], nyq, jnp.zeros((6, u32.shape[1]), F32)], axis=0)


def _hy_pre(z3, short_w, short_b, hy_off, hw):
    bsz, seq_len, _ = z3.shape
    half = seq_len // 2
    tc = LANES
    assert hy_off % tc == 0 and hw % tc == 0
    ob, nc = hy_off // tc, hw // tc
    zspec = lambda part: pl.BlockSpec((1, seq_len, tc), lambda b, c: (b, 0, ob + part * nc + c))
    wspec = lambda part: pl.BlockSpec((3, tc), lambda b, c: (0, part * nc + c))
    bspec = lambda part: pl.BlockSpec((1, tc), lambda b, c: (0, part * nc + c))
    ospec = pl.BlockSpec((1, half, tc), lambda b, c: (b, 0, c))
    sw = short_w.astype(F32)
    sb = short_b.reshape(1, -1).astype(F32)
    return pl.pallas_call(
        _hy_pre_kernel,
        grid=(bsz, nc),
        in_specs=[zspec(0), zspec(1), zspec(2), wspec(0), wspec(1), wspec(2), bspec(0), bspec(1), bspec(2)],
        out_specs=[ospec] * 6 + [pl.BlockSpec((1, 8, tc), lambda b, c: (b, 0, c))],
        out_shape=[jax.ShapeDtypeStruct((bsz, half, hw), BF16)] * 6 + [jax.ShapeDtypeStruct((bsz, 8, hw), F32)],
        compiler_params=_params(("parallel", "parallel")),
        name="hyena_short_conv",
    )(z3, z3, z3, sw, sw, sw, sb, sb, sb)


def _filter_mlp_kernel(zf_ref, w1_ref, b1_ref, f1_ref, w2_ref, b2_ref, f2_ref, w3f_ref, w3b_ref,
                       decf_ref, decb_ref, hs_ref, hd_ref):
    tl = zf_ref.shape[0]
    hp = lax.Precision.HIGHEST
    zf = zf_ref[...]
    h = jnp.sin(f1_ref[...] * (jnp.dot(zf, w1_ref[...], precision=hp, preferred_element_type=F32) + b1_ref[...]))
    h = jnp.sin(f2_ref[...] * (jnp.dot(h, w2_ref[...], precision=hp, preferred_element_type=F32) + b2_ref[...]))
    t = zf[:, 0:1]
    fwd = jnp.dot(h, w3f_ref[...], precision=hp, preferred_element_type=F32) * jnp.exp(-t * decf_ref[...])
    bwd = jnp.dot(h, w3b_ref[...], precision=hp, preferred_element_type=F32) * jnp.exp(-t * decb_ref[...])
    pos = pl.program_id(0) * tl + lax.broadcasted_iota(jnp.int32, (tl, 1), 0)
    bwd = jnp.where(pos == 0, 0.0, bwd)
    hs_ref[...] = (fwd + bwd).astype(hs_ref.dtype)
    hd_ref[...] = (fwd - bwd).astype(hd_ref.dtype)


def _pad2(a, rows, cols):
    return jnp.pad(a.astype(F32), ((0, rows - a.shape[0]), (0, cols - a.shape[1])))


def _filter_mlp(seq_len, w1, b1, f1, w2, b2, f2, w3, decay):
    hw = decay.shape[1]
    pos = jnp.arange(seq_len, dtype=F32)
    t = pos / (seq_len - 1)
    bands = jnp.linspace(1e-4, FILTER_BANDS - 1, FILTER_BANDS, dtype=F32)
    ang = (2.0 * math.pi / seq_len) * pos[:, None] * bands[None, :]
    zf = jnp.concatenate([t[:, None], jnp.cos(ang), -jnp.sin(ang)], axis=-1)
    emb, hid = w1.shape
    assert emb <= LANES and hid <= LANES
    zf = _pad2(zf, seq_len, LANES)
    row = lambda v: _pad2(v.reshape(1, -1), 1, LANES)
    w3p = _pad2(w3, LANES, 2 * hw)
    dec = decay.reshape(1, 2 * hw).astype(F32)
    tl = _pick(seq_len, 512, 8)
    tc = _pick(hw, 2048, LANES)
    nc = hw // tc
    small = lambda shape: pl.BlockSpec(shape, lambda i, c: (0, 0))
    return pl.pallas_call(
        _filter_mlp_kernel,
        grid=(seq_len // tl, nc),
        in_specs=[pl.BlockSpec((tl, LANES), lambda i, c: (i, 0)),
                  small((LANES, LANES)), small((1, LANES)), small((1, LANES)),
                  small((LANES, LANES)), small((1, LANES)), small((1, LANES)),
                  pl.BlockSpec((LANES, tc), lambda i, c: (0, c)),
                  pl.BlockSpec((LANES, tc), lambda i, c: (0, nc + c)),
                  pl.BlockSpec((1, tc), lambda i, c: (0, c)),
                  pl.BlockSpec((1, tc), lambda i, c: (0, nc + c))],
        out_specs=[pl.BlockSpec((tl, tc), lambda i, c: (i, c))] * 2,
        out_shape=[jax.ShapeDtypeStruct((seq_len, hw), BF16)] * 2,
        compiler_params=_params(("parallel", "parallel")),
        name="hyena_filter_mlp",
    )(zf, _pad2(w1, LANES, LANES), row(b1), row(f1), _pad2(w2, LANES, LANES), row(b2), row(f2),
      w3p, w3p, dec, dec)


def _trig_kernel(cb_ref, sb_ref, ck_ref, sk_ref, c_ref, s_ref, *, nyquist_row):
    tk, n_cols = cb_ref.shape
    cb, sb = cb_ref[...], sb_ref[...]
    ck, sk = ck_ref[0], sk_ref[0]
    sn = sk * cb + ck * sb
    if nyquist_row:
        k = pl.program_id(0) * tk + lax.broadcasted_iota(jnp.int32, (tk, 1), 0)
        n = lax.broadcasted_iota(jnp.int32, (1, n_cols), 1)
        sn = jnp.where(k == 0, jnp.where(n % 2 == 1, -1.0, 1.0), sn)
    c_ref[...] = (ck * cb - sk * sb).astype(c_ref.dtype)
    s_ref[...] = sn.astype(s_ref.dtype)


def _trig_matrices(n, n2, base_idx, tile_idx, name, n_cols=None, nyquist_row=False):
    tk = _pick(n, 256, HALO)
    n_i = n // tk
    rows, n = n, n_cols or n
    col = jnp.arange(n, dtype=jnp.int32)[None, :]

    def tables(idx):
        ang = (idx % n2).astype(F32) * (2.0 * math.pi / n2)
        return jnp.cos(ang), jnp.sin(ang)

    cb, sb = tables(base_idx(jnp.arange(tk, dtype=jnp.int32)[:, None], col))
    ck, sk = tables(tile_idx(jnp.arange(n_i, dtype=jnp.int32)[:, None] * tk, col))
    base = pl.BlockSpec((tk, n), lambda i: (0, 0))
    rowt = pl.BlockSpec((1, 1, n), lambda i: (i, 0, 0))
    ospec = pl.BlockSpec((tk, n), lambda i: (i, 0))
    return pl.pallas_call(
        functools.partial(_trig_kernel, nyquist_row=nyquist_row),
        grid=(n_i,),
        in_specs=[base, base, rowt, rowt],
        out_specs=[ospec] * 2,
        out_shape=[jax.ShapeDtypeStruct((rows, n), BF16)] * 2,
        compiler_params=_params(("parallel",)),
        name=name,
    )(cb, sb, ck.reshape(n_i, 1, n), sk.reshape(n_i, 1, n))


def _spectrum_mats(seq_len):
    half, n2 = seq_len // 2, 2 * seq_len
    even = _trig_matrices(half, n2, lambda d, c: 2 * d * c, lambda r, c: 2 * r * c, "dft_even_rows",
                          n_cols=seq_len, nyquist_row=True)
    odd = _trig_matrices(half, n2, lambda d, c: (2 * d + 1) * c, lambda r, c: 2 * r * c, "dft_odd_rows",
                         n_cols=seq_len)
    return even, odd


def _folded_dft_mats(seq_len):
    half, n2 = seq_len // 2, 2 * seq_len
    ce, se = _trig_matrices(half, n2, lambda d, c: 2 * d * c, lambda r, c: 2 * r * c, "dft_even_bins")
    co, so = _trig_matrices(half, n2, lambda d, c: (2 * d + 1) * c, lambda r, c: 2 * r * c, "dft_odd_bins")
    cot, sot = _trig_matrices(half, n2, lambda d, c: (2 * c + 1) * d, lambda r, c: (2 * c + 1) * r,
                              "dft_odd_bins_t")
    return ce, se, co, so, cot, sot


def _filter_spectrum_kernel(fc_ref, fs_ref, hs_ref, hd_ref, a_ref, d_ref, *, seq_len, even):
    tm = fc_ref.shape[0]
    hs = hs_ref[...]
    scale = 2.0 / (2 * seq_len)
    if even:
        row0 = (pl.program_id(0) * tm + lax.broadcasted_iota(jnp.int32, (tm, 1), 0)) == 0
        scale = jnp.where(row0, 1.0 / (2 * seq_len), scale)
    a_ref[...] = scale * jnp.dot(fc_ref[...], hs, preferred_element_type=F32)
    d_ref[...] = scale * jnp.dot(fs_ref[...], hd_ref[...], preferred_element_type=F32)
    if even:
        @pl.when(pl.program_id(0) == 0)
        def _():
            nyq = jnp.dot(fs_ref[0:HALO, :], hs, preferred_element_type=F32)[0:1]
            d_ref[0:1, :] = nyq * (1.0 / (2 * seq_len))


def _filter_spectrum(fc, fs, hs, hd, even):
    seq_len, hw = hs.shape
    half = fc.shape[0]
    tm = _pick(half, 512, HALO)
    tn = _pick(hw, 512, LANES)
    fspec = pl.BlockSpec((tm, seq_len), lambda i, j: (i, 0))
    hspec = pl.BlockSpec((seq_len, tn), lambda i, j: (0, j))
    ospec = pl.BlockSpec((tm, tn), lambda i, j: (i, j))
    return pl.pallas_call(
        functools.partial(_filter_spectrum_kernel, seq_len=seq_len, even=even),
        grid=(half // tm, hw // tn),
        in_specs=[fspec, fspec, hspec, hspec],
        out_specs=[ospec, ospec],
        out_shape=[jax.ShapeDtypeStruct((half, hw), F32)] * 2,
        compiler_params=_params(("parallel", "parallel")),
        name="hyena_filter_spectrum",
    )(fc, fs, hs, hd)


def _alt_sign(idx):
    return jnp.where(idx % 2 == 1, -1.0, 1.0)


def _dft_fwd_kernel(ce_ref, so_ref, co_ref, se_ref, e_ref, o_ref, side_ref, ae_ref, de_ref, ao_ref, do_ref,
                    w1e_ref, w2e_ref, w1o_ref, w2o_ref):
    tm = ce_ref.shape[0]
    e, o = e_ref[0], o_ref[0]
    pe_all = jnp.dot(ce_ref[...], e, preferred_element_type=F32)
    qo_all = jnp.dot(so_ref[...], e, preferred_element_type=F32)
    po_all = jnp.dot(co_ref[...], o, preferred_element_type=F32)
    qe_all = jnp.dot(se_ref[...], o, preferred_element_type=F32)
    mid = side_ref[0, 0:1, :]
    nyq = side_ref[0, 1:2, :]
    sub = _pick(tm, EPILOGUE_ROWS, 8)
    for r0 in range(0, tm, sub):
        rs = slice(r0, r0 + sub)
        m = pl.program_id(0) * tm + r0 + lax.broadcasted_iota(jnp.int32, (sub, 1), 0)
        corr = _alt_sign(m) * mid
        row0 = m == 0
        pe, qe, a, d = pe_all[rs] + corr, qe_all[rs], ae_ref[rs, :], de_ref[rs, :]
        qd = jnp.where(row0, nyq, qe) * d
        w1e_ref[0, rs, :] = (pe * a - jnp.where(row0, 0.0, qd)).astype(w1e_ref.dtype)
        w2e_ref[0, rs, :] = jnp.where(row0, qd, pe * d + qe * a).astype(w2e_ref.dtype)
        po, qo, a, d = po_all[rs], qo_all[rs] + corr, ao_ref[rs, :], do_ref[rs, :]
        w1o_ref[0, rs, :] = (po * a - qo * d).astype(w1o_ref.dtype)
        w2o_ref[0, rs, :] = (po * d + qo * a).astype(w2o_ref.dtype)


def _dft_fwd(mats, e, o, side, ae, de, ao, do):
    ce, se, co, so, _, _ = mats
    bsz, half, hw = e.shape
    tm = _pick(half, 512, HALO)
    tn = _pick(hw, 512, LANES)
    fspec = pl.BlockSpec((tm, half), lambda i, j, b: (i, 0))
    uspec = pl.BlockSpec((1, half, tn), lambda i, j, b: (b, 0, j))
    kspec = pl.BlockSpec((tm, tn), lambda i, j, b: (i, j))
    ospec = pl.BlockSpec((1, tm, tn), lambda i, j, b: (b, i, j))
    return pl.pallas_call(
        _dft_fwd_kernel,
        grid=(half // tm, hw // tn, bsz),
        in_specs=[fspec] * 4 + [uspec, uspec, pl.BlockSpec((1, 8, tn), lambda i, j, b: (b, 0, j))] + [kspec] * 4,
        out_specs=[ospec] * 4,
        out_shape=[jax.ShapeDtypeStruct((bsz, half, hw), BF16)] * 4,
        compiler_params=_params(("parallel", "parallel", "parallel")),
        name="hyena_dft_forward",
    )(ce, so, co, se, e, o, side, ae, de, ao, do)


def _dft_inv_kernel(ce_ref, sot_ref, cot_ref, se_ref, alt_ref, w1e_ref, w2e_ref, w1o_ref, w2o_ref,
                    ua_ref, ub_ref, xa_ref, xb_ref, skip_ref, oa_ref, ob_ref):
    tm = oa_ref.shape[1]
    half = w1e_ref.shape[1]
    w1e, w2o = w1e_ref[0], w2o_ref[0]
    s1_all = (jnp.dot(ce_ref[...], w1e, preferred_element_type=F32)
              + jnp.dot(sot_ref[...], w2o, preferred_element_type=F32))
    s2_all = (jnp.dot(cot_ref[...], w1o_ref[0], preferred_element_type=F32)
              + jnp.dot(se_ref[...], w2e_ref[0], preferred_element_type=F32))
    nyq = w2e_ref[0, 0:1, :].astype(F32)
    skip = skip_ref[...]
    sub = _pick(tm, EPILOGUE_ROWS, 8)
    for r0 in range(0, tm, sub):
        rs = slice(r0, r0 + sub)
        t = pl.program_id(0) * tm + r0 + lax.broadcasted_iota(jnp.int32, (sub, 1), 0)
        base = s1_all[rs] + _alt_sign(t) * nyq
        ya = base + s2_all[rs] + ua_ref[0, rs, :].astype(F32) * skip
        yb = base - s2_all[rs] + ub_ref[0, rs, :].astype(F32) * skip
        oa_ref[0, rs, :] = (ya * xa_ref[0, rs, :].astype(F32)).astype(oa_ref.dtype)
        ob_ref[0, rs, :] = (yb * xb_ref[0, rs, :].astype(F32)).astype(ob_ref.dtype)

    @pl.when(pl.program_id(0) == 0)
    def _():
        alt = alt_ref[...]
        mid = (jnp.dot(alt, w1e, preferred_element_type=F32) + jnp.dot(alt, w2o, preferred_element_type=F32))[0:1]
        y = mid + (1.0 if half % 2 == 0 else -1.0) * nyq + ub_ref[0, 0:1, :].astype(F32) * skip
        ob_ref[0, 0:1, :] = (y * xb_ref[0, 0:1, :].astype(F32)).astype(ob_ref.dtype)


def _dft_inv(mats, w1e, w2e, w1o, w2o, ua, ub, xa, xb, skip):
    ce, se, _, _, cot, sot = mats
    bsz, half, hw = ua.shape
    tm = _pick(half, 512, HALO)
    tn = _pick(hw, 512, LANES)
    alt = jnp.zeros((HALO, half), F32).at[0].set(_alt_sign(jnp.arange(half))).astype(BF16)
    fspec = pl.BlockSpec((tm, half), lambda i, j, b: (i, 0))
    wspec = pl.BlockSpec((1, half, tn), lambda i, j, b: (b, 0, j))
    tspec = pl.BlockSpec((1, tm, tn), lambda i, j, b: (b, i, j))
    return pl.pallas_call(
        _dft_inv_kernel,
        grid=(half // tm, hw // tn, bsz),
        in_specs=[fspec] * 4 + [pl.BlockSpec((HALO, half), lambda i, j, b: (0, 0))] + [wspec] * 4 + [tspec] * 4
                 + [pl.BlockSpec((1, tn), lambda i, j, b: (0, j))],
        out_specs=[tspec, tspec],
        out_shape=[jax.ShapeDtypeStruct((bsz, half, hw), BF16)] * 2,
        compiler_params=_params(("parallel", "parallel", "parallel")),
        name="hyena_dft_inverse",
    )(ce, sot, cot, se, alt, w1e, w2e, w1o, w2o, ua, ub, xa, xb, skip.reshape(1, hw).astype(F32))


def _unfold_kernel(oa_ref, ob_ref, o_ref):
    half = oa_ref.shape[1]
    o_ref[0, 0:half, :] = oa_ref[0]
    o_ref[0, half:2 * half, :] = _negate_index(ob_ref[0])


def _unfold(oa, ob):
    bsz, half, hw = oa.shape
    tc = _pick(hw, 512, LANES)
    ispec = pl.BlockSpec((1, half, tc), lambda b, c: (b, 0, c))
    return pl.pallas_call(
        _unfold_kernel,
        grid=(bsz, hw // tc),
        in_specs=[ispec, ispec],
        out_specs=pl.BlockSpec((1, 2 * half, tc), lambda b, c: (b, 0, c)),
        out_shape=jax.ShapeDtypeStruct((bsz, 2 * half, hw), oa.dtype),
        compiler_params=_params(("parallel", "parallel")),
        name="hyena_unfold",
    )(oa, ob)


def _ffn_up_kernel(x_hbm, xp_ref, xn_ref, g_ref, wg_ref, wv_ref, cw_ref, cb_ref, o_ref, h_ref, halo_ref,
                   x_buf, x_sem, *, seq_len, n_j):
    s = pl.program_id(0)
    tm = x_buf.shape[0]
    _norm_row_tile(x_hbm, x_buf, x_sem, g_ref, h_ref, s, n_j)

    @pl.when(s % n_j == 0)
    def _():
        g = g_ref[...]
        halo_ref[0:HALO] = _rms_scale(xp_ref[...], g).astype(halo_ref.dtype)
        halo_ref[HALO:2 * HALO] = _rms_scale(xn_ref[...], g).astype(halo_ref.dtype)

    h = h_ref[...]
    wg = wg_ref[...]
    gu = jnp.dot(h, wg, preferred_element_type=F32)
    val = jnp.dot(h, wv_ref[...], preferred_element_type=F32)
    hg = jnp.dot(halo_ref[...], wg, preferred_element_type=F32)
    row_start = (s // n_j) * tm
    above = jnp.where(row_start % seq_len == 0, 0.0, hg[HALO - 8:HALO])
    below = jnp.where((row_start + tm) % seq_len == 0, 0.0, hg[HALO:HALO + 8])
    cw = cw_ref[...]
    cb = cb_ref[...]
    sub = _pick(tm, EPILOGUE_ROWS, 8)
    ext = sub + 16
    for r0 in range(0, tm, sub):
        before = above if r0 == 0 else gu[r0 - 8:r0]
        after = below if r0 + sub == tm else gu[r0 + sub:r0 + sub + 8]
        g_ext = jnp.concatenate([before, gu[r0:r0 + sub], after], axis=0)
        g_m1 = pltpu.roll(g_ext, 1, 0)[8:8 + sub]
        g_p1 = pltpu.roll(g_ext, ext - 1, 0)[8:8 + sub]
        conv = g_m1 * cw[0:1] + g_ext[8:8 + sub] * cw[1:2] + g_p1 * cw[2:3] + cb
        gelu = 0.5 * conv * (1.0 + lax.erf(conv * math.sqrt(0.5)))
        o_ref[r0:r0 + sub, :] = (gelu * val[r0:r0 + sub]).astype(o_ref.dtype)


def _ffn_up(x, gain, w_up, conv_w, conv_b, seq_len):
    rows, d = x.shape
    d_ff = w_up.shape[1] // 2
    tm = _pick(seq_len, 1024, HALO)
    tn = _pick(d_ff, 256, LANES)
    n_i, n_j = rows // tm, d_ff // tn
    hb = tm // HALO
    n_halo = rows // HALO
    return pl.pallas_call(
        functools.partial(_ffn_up_kernel, seq_len=seq_len, n_j=n_j),
        grid=(n_i * n_j,),
        in_specs=[pl.BlockSpec(memory_space=pl.ANY),
                  pl.BlockSpec((HALO, d), lambda s: (jnp.maximum((s // n_j) * hb - 1, 0), 0)),
                  pl.BlockSpec((HALO, d), lambda s: (jnp.minimum((s // n_j + 1) * hb, n_halo - 1), 0)),
                  pl.BlockSpec((1, d), lambda s: (0, 0)),
                  pl.BlockSpec((d, tn), lambda s: (0, s % n_j)),
                  pl.BlockSpec((d, tn), lambda s: (0, n_j + s % n_j)),
                  pl.BlockSpec((3, tn), lambda s: (0, s % n_j)),
                  pl.BlockSpec((1, tn), lambda s: (0, s % n_j))],
        out_specs=pl.BlockSpec((tm, tn), lambda s: (s // n_j, s % n_j)),
        out_shape=jax.ShapeDtypeStruct((rows, d_ff), BF16),
        scratch_shapes=[pltpu.VMEM((tm, d), BF16), pltpu.VMEM((2 * HALO, d), BF16), pltpu.VMEM((tm, d), F32),
                        pltpu.SemaphoreType.DMA(())],
        compiler_params=_params(("arbitrary",)),
        name="convglu_up",
    )(x, x, x, gain.reshape(1, d).astype(F32), w_up, w_up, conv_w.astype(F32),
      conv_b.reshape(1, d_ff).astype(F32))


def _prep_layer(w):
    names = ("w_in", "w_attn_o", "w_hyena_o", "w_out", "w_up", "w_down", "w_ple_gate", "w_ple")
    return {k: w[k].astype(BF16) for k in names}


def _encoder_layer(x, p, tab, w, wp, seq_len):
    rows, d_model = x.shape
    bsz = rows // seq_len
    attn_w = w["w_attn_o"].shape[0]
    hw = w["w_hyena_o"].shape[0]
    in_cols = w["w_in"].shape[1]
    kv_w = (in_cols - attn_w - 3 * hw - 2 * d_model) // 2
    hy_off = attn_w + 2 * kv_w
    gate_off = hy_off + 3 * hw
    tm = _pick(seq_len, 1024, HALO)

    z = _fused_matmul([(x, wp["w_in"])], [], _ep_plain, BF16, in_cols, tm, _pick(in_cols, 1024, LANES),
                      "in_proj", norm_gain=w["g_mix"])
    attn = _attention(z, tab, w["attn_sink"], seq_len, attn_w, kv_w)

    ua, ub, xa, xb, e, o, side = _hy_pre(z.reshape(bsz, seq_len, in_cols), w["hy_short_w"], w["hy_short_b"],
                                         hy_off, hw)
    hs, hd = _filter_mlp(seq_len, w["hy_filt_w1"], w["hy_filt_b1"], w["hy_filt_f1"], w["hy_filt_w2"],
                         w["hy_filt_b2"], w["hy_filt_f2"], w["hy_filt_w3"], w["hy_decay"])
    (fce, fse), (fco, fso) = _spectrum_mats(seq_len)
    ae, de = _filter_spectrum(fce, fse, hs, hd, even=True)
    ao, do = _filter_spectrum(fco, fso, hs, hd, even=False)
    mats = _folded_dft_mats(seq_len)
    w1e, w2e, w1o, w2o = _dft_fwd(mats, e, o, side, ae, de, ao, do)
    oa, ob = _dft_inv(mats, w1e, w2e, w1o, w2o, ua, ub, xa, xb, w["hy_skip"])
    hy_out = _unfold(oa, ob).reshape(rows, hw)

    tn = _pick(math.gcd(d_model, gate_off), 1024, LANES)
    merged = _fused_matmul([(attn, wp["w_attn_o"]), (hy_out, wp["w_hyena_o"])],
                           [(z, "tile", gate_off), (z, "tile", gate_off + d_model)],
                           _ep_branch_merge, BF16, d_model, tm, tn, "branch_merge")
    x = _fused_matmul([(merged, wp["w_out"])], [(x, "tile", 0)], _ep_residual, F32, d_model, tm,
                      _pick(d_model, 1024, LANES), "out_proj")
    tn = _pick(d_model, 512, LANES)

    act = _ffn_up(x, w["g_ffn"], wp["w_up"], w["ffn_conv_w"], w["ffn_conv_b"], seq_len)
    x = _fused_matmul([(act, wp["w_down"])], [(x, "tile", 0)], _ep_residual, F32, d_model,
                      _pick(seq_len, 512, HALO), _pick(d_model, 512, LANES), "ffn_down")

    x = _fused_matmul([(x, wp["w_ple_gate"]), (p.astype(BF16), wp["w_ple"])], [(x, "tile", 0)],
                      _ep_ple, F32, d_model, tm, tn, "ple_gate", norm_gain=w["g_ple"])
    return x


_LAYER_WEIGHT_NAMES = ("g_mix", "w_in", "attn_sink", "hy_short_w", "hy_short_b", "hy_filt_w1", "hy_filt_b1",
                       "hy_filt_f1", "hy_filt_w2", "hy_filt_b2", "hy_filt_f2", "hy_filt_w3", "hy_decay",
                       "hy_skip", "w_attn_o", "w_hyena_o", "w_out", "g_ffn", "w_up", "ffn_conv_w", "ffn_conv_b",
                       "w_down", "g_ple", "w_ple_gate", "w_ple")


def _trunk(x, p, tab, layers, preps, g_final):
    bsz, seq_len, d_model = x.shape
    xf = x.reshape(bsz * seq_len, d_model)
    for l, (w, wp) in enumerate(zip(layers, preps)):
        xf = _encoder_layer(xf, p[l].reshape(bsz * seq_len, -1), tab, w, wp, seq_len)
    return _rmsnorm(xf, g_final, x.dtype).reshape(bsz, seq_len, d_model)


def kernel(x_prompt, x_sample, p_prompt, p_sample, rel_bias, g_mix, w_in, attn_sink, hy_short_w, hy_short_b, hy_filt_w1, hy_filt_b1, hy_filt_f1, hy_filt_w2, hy_filt_b2, hy_filt_f2, hy_filt_w3, hy_decay, hy_skip, w_attn_o, w_hyena_o, w_out, g_ffn, w_up, ffn_conv_w, ffn_conv_b, w_down, g_ple, w_ple_gate, w_ple, g_final):
    stacked = (g_mix, w_in, attn_sink, hy_short_w, hy_short_b, hy_filt_w1, hy_filt_b1, hy_filt_f1, hy_filt_w2,
               hy_filt_b2, hy_filt_f2, hy_filt_w3, hy_decay, hy_skip, w_attn_o, w_hyena_o, w_out, g_ffn, w_up,
               ffn_conv_w, ffn_conv_b, w_down, g_ple, w_ple_gate, w_ple)
    depth = g_mix.shape[0]
    layers = [dict(zip(_LAYER_WEIGHT_NAMES, [a[l] for a in stacked])) for l in range(depth)]
    preps = [_prep_layer(w) for w in layers]
    tab = _bias_table(rel_bias)
    y_prompt = _trunk(x_prompt, p_prompt, tab, layers, preps, g_final)
    y_sample = _trunk(x_sample, p_sample, tab, layers, preps, g_final)
    return (y_prompt, y_sample)
```
